```python
import jax, jax.numpy as jnp
from jax import lax
import numpy as np

D_MODEL = 2048
BATCH = 8
SEQ = 2048
DEPTH = 1
DEC_BATCH = 32
DEC_SEQ = 1
PAST_LEN = 16384
PAGE_SIZE = 128

HEAD_DIM = 128
N_ATTN_HEADS = 8
N_MLP_GROUPS = 8
ATTN_WIDTH = N_ATTN_HEADS * HEAD_DIM
MLP_WIDTH = N_MLP_GROUPS * HEAD_DIM
MIX_WIDTH = ATTN_WIDTH + MLP_WIDTH
CHUNK = 128
Q_BLOCK = 128
D_FF = 5504
D_PLE = 256
RMS_EPS = 1e-6
IN_COLS = 3 * ATTN_WIDTH + 2 * MLP_WIDTH + N_ATTN_HEADS

kernel_name = "hymba_fox_gmlp_macaron_decode_step"


def rmsnorm(x, w):
    xf = x.astype(jnp.float32)
    y = xf * lax.rsqrt(jnp.mean(xf * xf, axis=-1, keepdims=True) + RMS_EPS)
    return (y * w.astype(jnp.float32)).astype(x.dtype)


def swiglu(x, w_gate, w_up, w_down):
    return (jax.nn.silu(x @ w_gate) * (x @ w_up)) @ w_down


def gmlp_mix(u, v, spatial_w, spatial_b):
    B, L, G, C = v.shape
    Lp = -(-L // CHUNK) * CHUNK
    vp = jnp.pad(v, ((0, 0), (0, Lp - L), (0, 0), (0, 0)))
    vc = vp.reshape(B, Lp // CHUNK, CHUNK, G, C)
    causal = jnp.tril(jnp.ones((CHUNK, CHUNK), dtype=bool))
    w = jnp.where(causal[None], spatial_w, jnp.zeros_like(spatial_w))
    mixed = jnp.einsum("gts,bnsgc->bntgc", w, vc) + spatial_b.T[None, None, :, :, None]
    mixed = mixed.reshape(B, Lp, G, C)[:, :L]
    return u * mixed


def fox_prompt(q, k, v, logf):
    B, S, H, Dh = q.shape
    Ft = jnp.cumsum(logf, axis=1).transpose(0, 2, 1)
    kpos = jnp.arange(S)
    scale = Dh ** -0.5

    def block(i):
        start = i * Q_BLOCK
        qb = lax.dynamic_slice_in_dim(q, start, Q_BLOCK, axis=1)
        Fq = lax.dynamic_slice_in_dim(Ft, start, Q_BLOCK, axis=2)
        s = jnp.einsum("bqhd,bkhd->bhqk", qb, k, preferred_element_type=jnp.float32) * scale
        s = s + Fq[..., :, None] - Ft[..., None, :]
        qpos = start + jnp.arange(Q_BLOCK)
        s = jnp.where(qpos[:, None] >= kpos[None, :], s, -jnp.inf)
        pr = jax.nn.softmax(s, axis=-1).astype(v.dtype)
        return jnp.einsum("bhqk,bkhd->bqhd", pr, v)

    out = lax.map(block, jnp.arange(S // Q_BLOCK))
    return jnp.moveaxis(out, 0, 1).reshape(B, S, H, Dh)


def fox_sample(q, k_new, v_new, logf_new, k_past, v_past, logf_past):
    T = q.shape[1]
    P = k_past.shape[1]
    scale = q.shape[-1] ** -0.5
    F = jnp.cumsum(jnp.concatenate([logf_past.astype(jnp.float32), logf_new], axis=1), axis=1)
    F = F.transpose(0, 2, 1)
    Fq = F[..., P:]
    s_past = jnp.einsum("bqhd,bkhd->bhqk", q, k_past, preferred_element_type=jnp.float32) * scale
    s_past = s_past + Fq[..., :, None] - F[..., None, :P]
    s_new = jnp.einsum("bqhd,bkhd->bhqk", q, k_new, preferred_element_type=jnp.float32) * scale
    s_new = s_new + Fq[..., :, None] - F[..., None, P:]
    causal = jnp.arange(T)[:, None] >= jnp.arange(T)[None, :]
    s_new = jnp.where(causal, s_new, -jnp.inf)
    pr = jax.nn.softmax(jnp.concatenate([s_past, s_new], axis=-1), axis=-1).astype(v_new.dtype)
    return (jnp.einsum("bhqk,bkhd->bqhd", pr[..., :P], v_past.astype(v_new.dtype))
            + jnp.einsum("bhqk,bkhd->bqhd", pr[..., P:], v_new))


def mixer_projections(h, mix_norm, w_in, b_f, q_norm, k_norm, v_mlp_norm):
    B, L, _ = h.shape
    a = rmsnorm(h, mix_norm)
    z = a @ w_in
    cuts = [ATTN_WIDTH, 2 * ATTN_WIDTH, 3 * ATTN_WIDTH, 3 * ATTN_WIDTH + MLP_WIDTH,
            3 * ATTN_WIDTH + 2 * MLP_WIDTH]
    zq, zk, zv, zu, zvm, zf = jnp.split(z, cuts, axis=-1)
    q = rmsnorm(zq.reshape(B, L, N_ATTN_HEADS, HEAD_DIM), q_norm)
    k = rmsnorm(zk.reshape(B, L, N_ATTN_HEADS, HEAD_DIM), k_norm)
    v = zv.reshape(B, L, N_ATTN_HEADS, HEAD_DIM)
    logf = jax.nn.log_sigmoid(zf.astype(jnp.float32) + b_f.astype(jnp.float32))
    u = jax.nn.gelu(zu).reshape(B, L, N_MLP_GROUPS, HEAD_DIM)
    vm = rmsnorm(jax.nn.gelu(zvm).reshape(B, L, N_MLP_GROUPS, HEAD_DIM), v_mlp_norm)
    return q, k, v, logf, u, vm


def post_mixer(h, ple, ffn2_norm, ffn2_w_gate, ffn2_w_up, ffn2_w_down, ple_norm, w_ple_gate, w_ple_proj):
    h = h + 0.5 * swiglu(rmsnorm(h, ffn2_norm), ffn2_w_gate, ffn2_w_up, ffn2_w_down)
    gate = jax.nn.sigmoid(rmsnorm(h, ple_norm) @ w_ple_gate)
    return h + gate * (ple @ w_ple_proj)


def setup_inputs(seed: int = 0) -> dict:
    key = jax.random.key(seed)
    ks = iter(jax.random.split(key, 40))
    n_pages = PAST_LEN // PAGE_SIZE
    n_used = DEC_BATCH * n_pages
    n_pool = n_used + max(1, n_used // 4)
    f32 = jnp.float32

    def dense(shape, fan_in):
        return jax.random.normal(next(ks), shape, f32) * (fan_in ** -0.5)

    def gain(shape):
        return 1.0 + 0.02 * jax.random.normal(next(ks), shape, f32)

    x_prompt = jax.random.normal(next(ks), (BATCH, SEQ, D_MODEL), f32)
    x_sample = jax.random.normal(next(ks), (DEC_BATCH, DEC_SEQ, D_MODEL), f32)
    p_prompt = jax.random.normal(next(ks), (DEPTH, BATCH, SEQ, D_PLE), f32)
    p_sample = jax.random.normal(next(ks), (DEPTH, DEC_BATCH, DEC_SEQ, D_PLE), f32)
    cache_k = jax.random.normal(next(ks), (DEPTH, n_pool, PAGE_SIZE, N_ATTN_HEADS, HEAD_DIM), f32)
    cache_v = jax.random.normal(next(ks), (DEPTH, n_pool, PAGE_SIZE, N_ATTN_HEADS, HEAD_DIM), f32)
    cache_logf = jax.nn.log_sigmoid(
        3.5 + jax.random.normal(next(ks), (DEPTH, n_pool, PAGE_SIZE, N_ATTN_HEADS), f32))
    perm = jax.random.permutation(next(ks), n_pool)[:n_used]
    page_table = perm.reshape(DEC_BATCH, n_pages).astype(jnp.int32)

    return {
        "x_prompt": x_prompt,
        "x_sample": x_sample,
        "p_prompt": p_prompt,
        "p_sample": p_sample,
        "cache_k": cache_k,
        "cache_v": cache_v,
        "cache_logf": cache_logf,
        "page_table": page_table,
        "ffn1_norm": gain((DEPTH, D_MODEL)),
        "ffn1_w_gate": dense((DEPTH, D_MODEL, D_FF), D_MODEL),
        "ffn1_w_up": dense((DEPTH, D_MODEL, D_FF), D_MODEL),
        "ffn1_w_down": dense((DEPTH, D_FF, D_MODEL), D_FF),
        "mix_norm": gain((DEPTH, D_MODEL)),
        "w_in": dense((DEPTH, D_MODEL, IN_COLS), D_MODEL),
        "b_f": 2.0 + 3.0 * jax.random.uniform(next(ks), (DEPTH, N_ATTN_HEADS), f32),
        "q_norm": gain((DEPTH, HEAD_DIM)),
        "k_norm": gain((DEPTH, HEAD_DIM)),
        "v_mlp_norm": gain((DEPTH, N_MLP_GROUPS, HEAD_DIM)),
        "spatial_w": dense((DEPTH, N_MLP_GROUPS, CHUNK, CHUNK), CHUNK),
        "spatial_b": 1.0 + 0.1 * jax.random.normal(next(ks), (DEPTH, N_MLP_GROUPS, CHUNK), f32),
        "w_o": dense((DEPTH, MIX_WIDTH, D_MODEL), MIX_WIDTH),
        "ffn2_norm": gain((DEPTH, D_MODEL)),
        "ffn2_w_gate": dense((DEPTH, D_MODEL, D_FF), D_MODEL),
        "ffn2_w_up": dense((DEPTH, D_MODEL, D_FF), D_MODEL),
        "ffn2_w_down": dense((DEPTH, D_FF, D_MODEL), D_FF),
        "ple_norm": gain((DEPTH, D_MODEL)),
        "w_ple_gate": dense((DEPTH, D_MODEL, D_MODEL), D_MODEL),
        "w_ple_proj": dense((DEPTH, D_PLE, D_MODEL), D_PLE),
    }


def reference(x_prompt, x_sample, p_prompt, p_sample, cache_k, cache_v, cache_logf, page_table,
              ffn1_norm, ffn1_w_gate, ffn1_w_up, ffn1_w_down, mix_norm, w_in, b_f, q_norm, k_norm,
              v_mlp_norm, spatial_w, spatial_b, w_o, ffn2_norm, ffn2_w_gate, ffn2_w_up, ffn2_w_down,
              ple_norm, w_ple_gate, w_ple_proj):
    hp, hs = x_prompt, x_sample
    Bs = x_sample.shape[0]
    past = page_table.shape[1] * PAGE_SIZE
    kp_l, vp_l, fp_l, ks_l, vs_l, fs_l, sv_l = [], [], [], [], [], [], []
    for l in range(DEPTH):
        hp = hp + 0.5 * swiglu(rmsnorm(hp, ffn1_norm[l]), ffn1_w_gate[l], ffn1_w_up[l], ffn1_w_down[l])
        hs = hs + 0.5 * swiglu(rmsnorm(hs, ffn1_norm[l]), ffn1_w_gate[l], ffn1_w_up[l], ffn1_w_down[l])

        q, k, v, logf, u, vm = mixer_projections(hp, mix_norm[l], w_in[l], b_f[l], q_norm[l], k_norm[l],
                                                 v_mlp_norm[l])
        att = fox_prompt(q, k, v, logf)
        mlp = gmlp_mix(u, vm, spatial_w[l], spatial_b[l])
        B, L = hp.shape[:2]
        hp = hp + jnp.concatenate([att.reshape(B, L, ATTN_WIDTH), mlp.reshape(B, L, MLP_WIDTH)], -1) @ w_o[l]
        kp_l.append(k); vp_l.append(v); fp_l.append(logf)

        qs, kn, vn, fn, us, vms = mixer_projections(hs, mix_norm[l], w_in[l], b_f[l], q_norm[l], k_norm[l],
                                                    v_mlp_norm[l])
        k_past = cache_k[l][page_table].reshape(Bs, past, N_ATTN_HEADS, HEAD_DIM)
        v_past = cache_v[l][page_table].reshape(Bs, past, N_ATTN_HEADS, HEAD_DIM)
        f_past = cache_logf[l][page_table].reshape(Bs, past, N_ATTN_HEADS)
        att_s = fox_sample(qs, kn, vn, fn, k_past, v_past, f_past)
        mlp_s = gmlp_mix(us, vms, spatial_w[l], spatial_b[l])
        T = hs.shape[1]
        hs = hs + jnp.concatenate([att_s.reshape(Bs, T, ATTN_WIDTH), mlp_s.reshape(Bs, T, MLP_WIDTH)], -1) @ w_o[l]
        ks_l.append(kn); vs_l.append(vn); fs_l.append(fn); sv_l.append(vms)

        hp = post_mixer(hp, p_prompt[l], ffn2_norm[l], ffn2_w_gate[l], ffn2_w_up[l], ffn2_w_down[l],
                        ple_norm[l], w_ple_gate[l], w_ple_proj[l])
        hs = post_mixer(hs, p_sample[l], ffn2_norm[l], ffn2_w_gate[l], ffn2_w_up[l], ffn2_w_down[l],
                        ple_norm[l], w_ple_gate[l], w_ple_proj[l])

    k_prompt = jnp.stack(kp_l)
    v_prompt = jnp.stack(vp_l)
    logf_prompt = jnp.stack(fp_l)
    k_sample = jnp.stack(ks_l)
    v_sample = jnp.stack(vs_l)
    logf_sample = jnp.stack(fs_l)
    state_v_sample = jnp.stack(sv_l)
    return (hp, hs, k_prompt, v_prompt, logf_prompt, k_sample, v_sample, logf_sample, state_v_sample)
```

```python
import functools

import jax
import jax.numpy as jnp
from jax import lax
from jax.experimental import pallas as pl
from jax.experimental.pallas import tpu as pltpu

F32 = jnp.float32
BF16 = jnp.bfloat16

HEAD_DIM = 128
N_HEADS = 8
N_GROUPS = 8
ATTN_WIDTH = N_HEADS * HEAD_DIM
MLP_WIDTH = N_GROUPS * HEAD_DIM
CHUNK = 128
PAGE = 128
RMS_EPS = 1e-6
QK_SCALE = HEAD_DIM ** -0.5

V7X_VMEM_BYTES = 64 * 1024 * 1024
VMEM_LIMIT = V7X_VMEM_BYTES - 8 * 1024 * 1024

FF_TILE = 512
ROW_TILE = 512
PROJ_TILE = 256
ATT_TILE = 256


def _params(*sem):
    return pltpu.CompilerParams(dimension_semantics=sem, vmem_limit_bytes=VMEM_LIMIT)


def _resident(shape):
    zeros = (0,) * len(shape)
    return pl.BlockSpec(shape, lambda *_: zeros, pipeline_mode=pl.Buffered(1))


def _rms(x, w):
    return x * lax.rsqrt(jnp.mean(x * x, axis=-1, keepdims=True) + RMS_EPS) * w


def _dot(a, b):
    return jnp.dot(a, b, preferred_element_type=F32)


def _ffn_kernel(x_ref, nw_ref, wg_ref, wu_ref, wd_ref, o_ref, xn_ref, *, nf):
    f = pl.program_id(1)

    @pl.when(f == 0)
    def _():
        xn_ref[...] = _rms(x_ref[...], nw_ref[...]).astype(BF16)
        o_ref[...] = jnp.zeros_like(o_ref)

    xn = xn_ref[...]
    g = _dot(xn, wg_ref[...])
    u = _dot(xn, wu_ref[...])
    h = (jax.nn.silu(g) * u).astype(BF16)
    o_ref[...] += _dot(h, wd_ref[...])

    @pl.when(f == nf - 1)
    def _():
        o_ref[...] = x_ref[...] + 0.5 * o_ref[...]


def _ffn(x, nw, wg, wu, wd, tm):
    m, d = x.shape
    nf = wg.shape[1] // FF_TILE
    return pl.pallas_call(
        functools.partial(_ffn_kernel, nf=nf),
        grid=(m // tm, nf),
        in_specs=[
            pl.BlockSpec((tm, d), lambda i, f: (i, 0)),
            pl.BlockSpec((1, d), lambda i, f: (0, 0)),
            pl.BlockSpec((d, FF_TILE), lambda i, f: (0, f)),
            pl.BlockSpec((d, FF_TILE), lambda i, f: (0, f)),
            pl.BlockSpec((FF_TILE, d), lambda i, f: (f, 0)),
        ],
        out_specs=pl.BlockSpec((tm, d), lambda i, f: (i, 0)),
        out_shape=jax.ShapeDtypeStruct((m, d), F32),
        scratch_shapes=[pltpu.VMEM((tm, d), BF16)],
        compiler_params=_params("parallel", "arbitrary"),
        name="ffn",
    )(x, nw, wg, wu, wd)


def _group_rms(z, gain):
    outs = []
    for h in range(z.shape[1] // HEAD_DIM):
        sl = slice(h * HEAD_DIM, (h + 1) * HEAD_DIM)
        g = gain if gain.shape[1] == HEAD_DIM else gain[:, sl]
        outs.append(_rms(z[:, sl], g))
    return outs


def _proj_kernel(h_ref, nw_ref, w_ref, wf_ref, bf_ref, qn_ref, kn_ref, vmn_ref,
                 q_ref, k_ref, v_ref, kb_ref, vb_ref, lf_ref, u_ref, vm_ref):
    a = _rms(h_ref[...], nw_ref[...]).astype(BF16)
    aw = ATTN_WIDTH

    zq = _dot(a, w_ref[:, 0:aw])
    for h, qh in enumerate(_group_rms(zq, qn_ref[...])):
        q_ref[:, h * HEAD_DIM:(h + 1) * HEAD_DIM] = (qh * QK_SCALE).astype(BF16)

    zk = _dot(a, w_ref[:, aw:2 * aw])
    for h, kh in enumerate(_group_rms(zk, kn_ref[...])):
        sl = slice(h * HEAD_DIM, (h + 1) * HEAD_DIM)
        k_ref[:, sl] = kh
        kb_ref[:, sl] = kh.astype(BF16)

    zv = _dot(a, w_ref[:, 2 * aw:3 * aw])
    v_ref[...] = zv
    vb_ref[...] = zv.astype(BF16)

    zu = _dot(a, w_ref[:, 3 * aw:3 * aw + MLP_WIDTH])
    u_ref[...] = jax.nn.gelu(zu)

    zm = _dot(a, w_ref[:, 3 * aw + MLP_WIDTH:3 * aw + 2 * MLP_WIDTH])
    for g, vg in enumerate(_group_rms(jax.nn.gelu(zm), vmn_ref[...])):
        vm_ref[:, g * HEAD_DIM:(g + 1) * HEAD_DIM] = vg

    zf = _dot(a, wf_ref[...]) + bf_ref[...]
    lf_ref[...] = jnp.minimum(zf, 0.0) - jnp.log1p(jnp.exp(-jnp.abs(zf)))


def _proj(h, nw, w_main, w_f, b_f, qn, kn, vmn, tm):
    m, d = h.shape
    row = lambda w: pl.BlockSpec((tm, w), lambda i: (i, 0))
    small = lambda w: pl.BlockSpec((1, w), lambda i: (0, 0))
    out_w = [(ATTN_WIDTH, BF16), (ATTN_WIDTH, F32), (ATTN_WIDTH, F32), (ATTN_WIDTH, BF16),
             (ATTN_WIDTH, BF16), (HEAD_DIM, F32), (MLP_WIDTH, F32), (MLP_WIDTH, F32)]
    return pl.pallas_call(
        _proj_kernel,
        grid=(m // tm,),
        in_specs=[row(d), small(d), _resident(w_main.shape), _resident(w_f.shape),
                  small(HEAD_DIM), small(HEAD_DIM), small(HEAD_DIM), small(MLP_WIDTH)],
        out_specs=[row(w) for w, _ in out_w],
        out_shape=[jax.ShapeDtypeStruct((m, w), dt) for w, dt in out_w],
        compiler_params=_params("parallel"),
        name="proj",
    )(h, nw, w_main, w_f, b_f, qn, kn, vmn)


def _cumsum_kernel(lf_ref, f_ref, ft_ref):
    x = lf_ref[...]
    n = x.shape[0]
    row = lax.broadcasted_iota(jnp.int32, x.shape, 0)
    s = 1
    while s < n:
        x = x + jnp.where(row >= s, pltpu.roll(x, s, 0), 0.0)
        s *= 2
    f_ref[...] = x
    ft_ref[...] = x.T[:N_HEADS, :]


def _cumsum(lf, batch, seq):
    return pl.pallas_call(
        _cumsum_kernel,
        grid=(batch,),
        in_specs=[pl.BlockSpec((seq, HEAD_DIM), lambda b: (b, 0))],
        out_specs=[pl.BlockSpec((seq, HEAD_DIM), lambda b: (b, 0)),
                   pl.BlockSpec((None, N_HEADS, seq), lambda b: (b, 0, 0))],
        out_shape=[jax.ShapeDtypeStruct((batch * seq, HEAD_DIM), F32),
                   jax.ShapeDtypeStruct((batch, N_HEADS, seq), F32)],
        compiler_params=_params("parallel"),
        name="cumsum",
    )(lf)


def _attn_kernel(q_ref, k_ref, v_ref, fc_ref, fr_ref, o_ref, *, t):
    h = pl.program_id(1)
    i = pl.program_id(2)
    q = q_ref[...]
    lane = lax.broadcasted_iota(jnp.int32, (t, HEAD_DIM), 1)
    fq = jnp.sum(jnp.where(lane == h, fc_ref[...], 0.0), axis=-1, keepdims=True)

    def block(j):
        rows = pl.ds(pl.multiple_of(j * t, t), t)
        s = lax.dot_general(q, k_ref[rows, :], (((1,), (1,)), ((), ())),
                            preferred_element_type=F32)
        return s + (fq - fr_ref[pl.ds(j, 1), :]), v_ref[rows, :]

    s, v = block(i)
    r = lax.broadcasted_iota(jnp.int32, (t, t), 0)
    c = lax.broadcasted_iota(jnp.int32, (t, t), 1)
    s = jnp.where(r >= c, s, -jnp.inf)
    m = jnp.max(s, axis=-1, keepdims=True)
    p = jnp.exp(s - m)
    l = jnp.sum(p, axis=-1, keepdims=True)
    acc = _dot(p.astype(BF16), v)

    def body(j, carry):
        m, l, acc = carry
        s, v = block(j)
        m_new = jnp.maximum(m, jnp.max(s, axis=-1, keepdims=True))
        alpha = jnp.exp(m - m_new)
        p = jnp.exp(s - m_new)
        l = alpha * l + jnp.sum(p, axis=-1, keepdims=True)
        acc = alpha * acc + _dot(p.astype(BF16), v)
        return m_new, l, acc

    m, l, acc = lax.fori_loop(0, i, body, (m, l, acc))
    o_ref[...] = (acc / l).astype(o_ref.dtype)


def _attn(q, kb, vb, fcol, frow, batch, seq):
    t = ATT_TILE
    nq = seq // t
    frow = frow.reshape(batch, N_HEADS, nq, t)
    return pl.pallas_call(
        functools.partial(_attn_kernel, t=t),
        grid=(batch, N_HEADS, nq),
        in_specs=[
            pl.BlockSpec((t, HEAD_DIM), lambda b, h, i: (b * nq + i, h)),
            pl.BlockSpec((seq, HEAD_DIM), lambda b, h, i: (b, h)),
            pl.BlockSpec((seq, HEAD_DIM), lambda b, h, i: (b, h)),
            pl.BlockSpec((t, HEAD_DIM), lambda b, h, i: (b * nq + i, 0)),
            pl.BlockSpec((None, None, nq, t), lambda b, h, i: (b, h, 0, 0)),
        ],
        out_specs=pl.BlockSpec((t, HEAD_DIM), lambda b, h, i: (b * nq + i, h)),
        out_shape=jax.ShapeDtypeStruct((batch * seq, ATTN_WIDTH), BF16),
        compiler_params=_params("parallel", "parallel", "arbitrary"),
        name="attn",
    )(q, kb, vb, fcol, frow)


def _mixout_kernel(att_ref, u_ref, vm_ref, sw_ref, sbt_ref, wo_ref, h_ref, o_ref, mlp_ref):
    tm = att_ref.shape[0]
    r = lax.broadcasted_iota(jnp.int32, (CHUNK, CHUNK), 0)
    c = lax.broadcasted_iota(jnp.int32, (CHUNK, CHUNK), 1)
    for g in range(N_GROUPS):
        w = jnp.where(r >= c, sw_ref[g], 0.0).astype(BF16)
        bias = sbt_ref[:, g:g + 1]
        cols = slice(g * HEAD_DIM, (g + 1) * HEAD_DIM)
        for n in range(tm // CHUNK):
            rows = slice(n * CHUNK, (n + 1) * CHUNK)
            mixed = _dot(w, vm_ref[rows, cols].astype(BF16)) + bias
            mlp_ref[rows, cols] = (u_ref[rows, cols] * mixed).astype(BF16)
    o_ref[...] = (h_ref[...] + _dot(att_ref[...], wo_ref[0:ATTN_WIDTH, :])
                  + _dot(mlp_ref[...], wo_ref[ATTN_WIDTH:, :]))


def _mixout(att, u, vm, sw, sbt, wo, h, tm):
    m, d = h.shape
    row = lambda w: pl.BlockSpec((tm, w), lambda i: (i, 0))
    return pl.pallas_call(
        _mixout_kernel,
        grid=(m // tm,),
        in_specs=[row(ATTN_WIDTH), row(MLP_WIDTH), row(MLP_WIDTH), _resident(sw.shape),
                  _resident(sbt.shape), _resident(wo.shape), row(d)],
        out_specs=row(d),
        out_shape=jax.ShapeDtypeStruct((m, d), F32),
        scratch_shapes=[pltpu.VMEM((tm, MLP_WIDTH), BF16)],
        compiler_params=_params("parallel"),
        name="mixout",
    )(att, u, vm, sw, sbt, wo, h)


def _mixout_one_kernel(att_ref, u_ref, vm_ref, w00_ref, b0_ref, wo_ref, h_ref, o_ref):
    mlp = u_ref[...] * (w00_ref[...] * vm_ref[...] + b0_ref[...])
    o_ref[...] = (h_ref[...] + _dot(att_ref[...].astype(BF16), wo_ref[0:ATTN_WIDTH, :])
                  + _dot(mlp.astype(BF16), wo_ref[ATTN_WIDTH:, :]))


def _mixout_one(att, u, vm, w00, b0, wo, h):
    m, d = h.shape
    full = lambda a: pl.BlockSpec(a.shape, lambda i: (0,) * a.ndim)
    args = (att, u, vm, w00, b0, wo, h)
    return pl.pallas_call(
        _mixout_one_kernel,
        grid=(1,),
        in_specs=[full(a) for a in args],
        out_specs=pl.BlockSpec((m, d), lambda i: (0, 0)),
        out_shape=jax.ShapeDtypeStruct((m, d), F32),
        compiler_params=_params("arbitrary"),
        name="mixout_one",
    )(*args)


def _head_expand():
    hh = lax.broadcasted_iota(jnp.int32, (N_HEADS, ATTN_WIDTH), 0)
    ll = lax.broadcasted_iota(jnp.int32, (N_HEADS, ATTN_WIDTH), 1)
    return (ll // HEAD_DIM == hh).astype(F32)


def _expand(x, e):
    return jnp.dot(x, e, preferred_element_type=F32, precision=lax.Precision.HIGHEST)


def _decode_kernel(pt_ref, qbd_ref, kn_ref, vn_ref, lfn_ref, ck_ref, cv_ref, clf_ref,
                   o_ref, m_ref, l_ref, acc_ref, c_ref, *, n_pages):
    p = pl.program_id(1)
    e = _head_expand()
    qbd = qbd_ref[...]

    @pl.when(p == 0)
    def _():
        m_ref[...] = _dot(kn_ref[...].astype(BF16), qbd)
        l_ref[...] = jnp.ones_like(l_ref)
        acc_ref[...] = vn_ref[...]
        c_ref[...] = lfn_ref[...]

    lf = clf_ref[...]
    row = lax.broadcasted_iota(jnp.int32, lf.shape, 0)
    y = lf
    s = 1
    while s < PAGE:
        y = y + jnp.where(row + s < PAGE, pltpu.roll(y, PAGE - s, 0), 0.0)
        s *= 2
    carry = c_ref[...]
    decay = y - lf + carry
    c_ref[...] = carry + y[0:1, :]

    logits = _dot(ck_ref[...].astype(BF16), qbd) + decay
    m_old = m_ref[...]
    m_new = jnp.maximum(m_old, jnp.max(logits, axis=0, keepdims=True))
    alpha = jnp.exp(m_old - m_new)
    pr = jnp.exp(logits - m_new)
    l_ref[...] = alpha * l_ref[...] + jnp.sum(pr, axis=0, keepdims=True)
    m_ref[...] = m_new
    pr_wide = _dot(pr.astype(BF16), e.astype(BF16))
    contrib = jnp.sum(pr_wide * cv_ref[...], axis=0, keepdims=True)
    acc_ref[...] = acc_ref[...] * _expand(alpha, e) + contrib

    @pl.when(p == n_pages - 1)
    def _():
        o_ref[...] = acc_ref[...] / _expand(l_ref[...], e)


def _decode(page_table, qbd, kn, vn, lfn, ck, cv, clf):
    nb, n_pages = page_table.shape
    w = ATTN_WIDTH
    page = lambda b, p, pt: (pt[b, n_pages - 1 - p], 0, 0)
    per_b = lambda b, p, pt: (b, 0, 0)
    grid_spec = pltpu.PrefetchScalarGridSpec(
        num_scalar_prefetch=1,
        grid=(nb, n_pages),
        in_specs=[
            pl.BlockSpec((None, w, N_HEADS), per_b),
            pl.BlockSpec((None, 1, w), per_b),
            pl.BlockSpec((None, 1, w), per_b),
            pl.BlockSpec((None, 1, N_HEADS), per_b),
            pl.BlockSpec((None, PAGE, w), page),
            pl.BlockSpec((None, PAGE, w), page),
            pl.BlockSpec((None, PAGE, N_HEADS), page),
        ],
        out_specs=pl.BlockSpec((None, 1, w), per_b),
        scratch_shapes=[pltpu.VMEM((1, N_HEADS), F32), pltpu.VMEM((1, N_HEADS), F32),
                        pltpu.VMEM((1, w), F32), pltpu.VMEM((1, N_HEADS), F32)],
    )
    return pl.pallas_call(
        functools.partial(_decode_kernel, n_pages=n_pages),
        grid_spec=grid_spec,
        out_shape=jax.ShapeDtypeStruct((nb, 1, w), F32),
        compiler_params=_params("parallel", "arbitrary"),
        name="decode",
    )(page_table, qbd, kn, vn, lfn, ck, cv, clf)


def _ple_kernel(h_ref, p_ref, nw_ref, wg_ref, wp_ref, o_ref):
    h = h_ref[...]
    gate = jax.nn.sigmoid(_dot(_rms(h, nw_ref[...]).astype(BF16), wg_ref[...]))
    o_ref[...] = h + gate * _dot(p_ref[...].astype(BF16), wp_ref[...])


def _ple(h, p, nw, wg, wp, tm):
    m, d = h.shape
    return pl.pallas_call(
        _ple_kernel,
        grid=(m // tm,),
        in_specs=[pl.BlockSpec((tm, d), lambda i: (i, 0)),
                  pl.BlockSpec((tm, p.shape[1]), lambda i: (i, 0)),
                  pl.BlockSpec((1, d), lambda i: (0, 0)),
                  _resident(wg.shape), _resident(wp.shape)],
        out_specs=pl.BlockSpec((tm, d), lambda i: (i, 0)),
        out_shape=jax.ShapeDtypeStruct((m, d), F32),
        compiler_params=_params("parallel"),
        name="ple",
    )(h, p, nw, wg, wp)


def _pad_cols(w, mult):
    return jnp.pad(w, ((0, 0), (0, (-w.shape[1]) % mult)))


def kernel(x_prompt, x_sample, p_prompt, p_sample, cache_k, cache_v, cache_logf, page_table,
           ffn1_norm, ffn1_w_gate, ffn1_w_up, ffn1_w_down, mix_norm, w_in, b_f, q_norm, k_norm,
           v_mlp_norm, spatial_w, spatial_b, w_o, ffn2_norm, ffn2_w_gate, ffn2_w_up, ffn2_w_down,
           ple_norm, w_ple_gate, w_ple_proj):
    batch, seq, d = x_prompt.shape
    nb, dec_seq, _ = x_sample.shape
    depth = ffn1_norm.shape[0]
    assert depth == 1 and dec_seq == 1
    assert seq % ATT_TILE == 0 and (batch * seq) % ROW_TILE == 0
    n_pool = cache_k.shape[1]
    l = 0

    def ffn_weights(wg, wu, wd):
        wd = jnp.pad(wd, ((0, (-wd.shape[0]) % FF_TILE), (0, 0)))
        return (_pad_cols(wg, FF_TILE).astype(BF16), _pad_cols(wu, FF_TILE).astype(BF16),
                wd.astype(BF16))

    ffn1 = (ffn1_norm[l][None],) + ffn_weights(ffn1_w_gate[l], ffn1_w_up[l], ffn1_w_down[l])
    ffn2 = (ffn2_norm[l][None],) + ffn_weights(ffn2_w_gate[l], ffn2_w_up[l], ffn2_w_down[l])
    n_main = 3 * ATTN_WIDTH + 2 * MLP_WIDTH
    proj = (mix_norm[l][None], w_in[l][:, :n_main].astype(BF16),
            _pad_cols(w_in[l][:, n_main:], HEAD_DIM).astype(BF16),
            _pad_cols(b_f[l][None], HEAD_DIM), q_norm[l][None], k_norm[l][None],
            v_mlp_norm[l].reshape(1, MLP_WIDTH))
    wo = w_o[l].astype(BF16)
    ple = (ple_norm[l][None], w_ple_gate[l].astype(BF16), w_ple_proj[l].astype(BF16))

    m = batch * seq
    hp = _ffn(x_prompt.reshape(m, d), *ffn1, tm=ROW_TILE)
    q, k, v, kb, vb, lf, u, vm = _proj(hp, *proj, tm=PROJ_TILE)
    fcol, frow = _cumsum(lf, batch, seq)
    att = _attn(q, kb, vb, fcol, frow, batch, seq)
    hp = _mixout(att, u, vm, spatial_w[l], spatial_b[l].T, wo, hp, tm=ROW_TILE)
    hp = _ffn(hp, *ffn2, tm=ROW_TILE)
    hp = _ple(hp, p_prompt[l].reshape(m, -1), *ple, tm=ROW_TILE)

    hs = _ffn(x_sample.reshape(nb, d), *ffn1, tm=nb)
    qs, ks, vs, _, _, lfs, us, vms = _proj(hs, *proj, tm=nb)
    eye = jnp.eye(N_HEADS, dtype=BF16)
    qbd = (qs.reshape(nb, N_HEADS, HEAD_DIM, 1) * eye[None, :, None, :]).reshape(
        nb, ATTN_WIDTH, N_HEADS)
    att_s = _decode(page_table, qbd, ks.reshape(nb, 1, -1), vs.reshape(nb, 1, -1),
                    lfs[:, :N_HEADS].reshape(nb, 1, N_HEADS),
                    cache_k[l].reshape(n_pool, PAGE, ATTN_WIDTH),
                    cache_v[l].reshape(n_pool, PAGE, ATTN_WIDTH), cache_logf[l])
    w00 = jnp.repeat(spatial_w[l][:, 0, 0], HEAD_DIM)[None]
    b0 = jnp.repeat(spatial_b[l][:, 0], HEAD_DIM)[None]
    hs = _mixout_one(att_s.reshape(nb, ATTN_WIDTH), us, vms, w00, b0, wo, hs)
    hs = _ffn(hs, *ffn2, tm=nb)
    hs = _ple(hs, p_sample[l].reshape(nb, -1), *ple, tm=nb)

    heads = (N_HEADS, HEAD_DIM)
    return (hp.reshape(batch, seq, d), hs.reshape(nb, 1, d),
            k.reshape(1, batch, seq, *heads), v.reshape(1, batch, seq, *heads),
            lf[:, :N_HEADS].reshape(1, batch, seq, N_HEADS),
            ks.reshape(1, nb, 1, *heads), vs.reshape(1, nb, 1, *heads),
            lfs[:, :N_HEADS].reshape(1, nb, 1, N_HEADS),
            vms.reshape(1, nb, 1, N_GROUPS, HEAD_DIM))
```

```python
import functools

import jax
import jax.numpy as jnp
from jax import lax
from jax.experimental import pallas as pl
from jax.experimental.pallas import tpu as pltpu

F32 = jnp.float32
BF16 = jnp.bfloat16

HEAD_DIM = 128
N_HEADS = 8
N_GROUPS = 8
ATTN_WIDTH = N_HEADS * HEAD_DIM
MLP_WIDTH = N_GROUPS * HEAD_DIM
CHUNK = 128
PAGE = 128
RMS_EPS = 1e-6
QK_SCALE = HEAD_DIM ** -0.5
LOG2E = 1.4426950408889634

V7X_VMEM_BYTES = 64 * 1024 * 1024
VMEM_LIMIT = V7X_VMEM_BYTES - 8 * 1024 * 1024

FF_TILE = 512
FFN_ROW_TILE = 1024
ROW_TILE = 512
PROJ_TILE = 256
ATT_TILE = 512
DEC_PAGES = 8


def _params(*sem):
    return pltpu.CompilerParams(dimension_semantics=sem, vmem_limit_bytes=VMEM_LIMIT)


def _resident(shape):
    zeros = (0,) * len(shape)
    return pl.BlockSpec(shape, lambda *_: zeros, pipeline_mode=pl.Buffered(1))


def _rms(x, w):
    return x * lax.rsqrt(jnp.mean(x * x, axis=-1, keepdims=True) + RMS_EPS) * w


def _dot(a, b):
    return jnp.dot(a, b, preferred_element_type=F32)


def _ffn_kernel(x_ref, nw_ref, wg_ref, wu_ref, wd_ref, o_ref, xn_ref, *, nf):
    f = pl.program_id(1)

    @pl.when(f == 0)
    def _():
        xn_ref[...] = _rms(x_ref[...], nw_ref[...]).astype(BF16)
        o_ref[...] = jnp.zeros_like(o_ref)

    xn = xn_ref[...]
    g = _dot(xn, wg_ref[...])
    u = _dot(xn, wu_ref[...])
    h = (jax.nn.silu(g) * u).astype(BF16)
    o_ref[...] += _dot(h, wd_ref[...])

    @pl.when(f == nf - 1)
    def _():
        o_ref[...] = x_ref[...] + 0.5 * o_ref[...]


def _ffn(x, nw, wg, wu, wd, tm):
    m, d = x.shape
    nf = wg.shape[1] // FF_TILE
    return pl.pallas_call(
        functools.partial(_ffn_kernel, nf=nf),
        grid=(m // tm, nf),
        in_specs=[
            pl.BlockSpec((tm, d), lambda i, f: (i, 0)),
            pl.BlockSpec((1, d), lambda i, f: (0, 0)),
            pl.BlockSpec((d, FF_TILE), lambda i, f: (0, f)),
            pl.BlockSpec((d, FF_TILE), lambda i, f: (0, f)),
            pl.BlockSpec((FF_TILE, d), lambda i, f: (f, 0)),
        ],
        out_specs=pl.BlockSpec((tm, d), lambda i, f: (i, 0)),
        out_shape=jax.ShapeDtypeStruct((m, d), F32),
        scratch_shapes=[pltpu.VMEM((tm, d), BF16)],
        compiler_params=_params("parallel", "arbitrary"),
        name="ffn",
    )(x, nw, wg, wu, wd)


def _group_rms(z, gain):
    outs = []
    for h in range(z.shape[1] // HEAD_DIM):
        sl = slice(h * HEAD_DIM, (h + 1) * HEAD_DIM)
        g = gain if gain.shape[1] == HEAD_DIM else gain[:, sl]
        outs.append(_rms(z[:, sl], g))
    return outs


def _proj_kernel(h_ref, nw_ref, w_ref, wf_ref, bf_ref, qn_ref, kn_ref, vmn_ref,
                 q_ref, k_ref, v_ref, kb_ref, vb_ref, lf_ref, u_ref, vm_ref, *, q_scale):
    a = _rms(h_ref[...], nw_ref[...]).astype(BF16)
    aw = ATTN_WIDTH

    zq = _dot(a, w_ref[:, 0:aw])
    for h, qh in enumerate(_group_rms(zq, qn_ref[...])):
        q_ref[:, h * HEAD_DIM:(h + 1) * HEAD_DIM] = (qh * q_scale).astype(BF16)

    zk = _dot(a, w_ref[:, aw:2 * aw])
    for h, kh in enumerate(_group_rms(zk, kn_ref[...])):
        sl = slice(h * HEAD_DIM, (h + 1) * HEAD_DIM)
        k_ref[:, sl] = kh
        kb_ref[:, sl] = kh.astype(BF16)

    zv = _dot(a, w_ref[:, 2 * aw:3 * aw])
    v_ref[...] = zv
    vb_ref[...] = zv.astype(BF16)

    zu = _dot(a, w_ref[:, 3 * aw:3 * aw + MLP_WIDTH])
    u_ref[...] = jax.nn.gelu(zu)

    zm = _dot(a, w_ref[:, 3 * aw + MLP_WIDTH:3 * aw + 2 * MLP_WIDTH])
    for g, vg in enumerate(_group_rms(jax.nn.gelu(zm), vmn_ref[...])):
        vm_ref[:, g * HEAD_DIM:(g + 1) * HEAD_DIM] = vg

    zf = _dot(a, wf_ref[...]) + bf_ref[...]
    lf_ref[...] = jnp.minimum(zf, 0.0) - jnp.log1p(jnp.exp(-jnp.abs(zf)))


def _proj(h, nw, w_main, w_f, b_f, qn, kn, vmn, tm, q_scale):
    m, d = h.shape
    row = lambda w: pl.BlockSpec((tm, w), lambda i: (i, 0))
    small = lambda w: pl.BlockSpec((1, w), lambda i: (0, 0))
    out_w = [(ATTN_WIDTH, BF16), (ATTN_WIDTH, F32), (ATTN_WIDTH, F32), (ATTN_WIDTH, BF16),
             (ATTN_WIDTH, BF16), (HEAD_DIM, F32), (MLP_WIDTH, F32), (MLP_WIDTH, F32)]
    return pl.pallas_call(
        functools.partial(_proj_kernel, q_scale=q_scale),
        grid=(m // tm,),
        in_specs=[row(d), small(d), _resident(w_main.shape), _resident(w_f.shape),
                  small(HEAD_DIM), small(HEAD_DIM), small(HEAD_DIM), small(MLP_WIDTH)],
        out_specs=[row(w) for w, _ in out_w],
        out_shape=[jax.ShapeDtypeStruct((m, w), dt) for w, dt in out_w],
        compiler_params=_params("parallel"),
        name="proj",
    )(h, nw, w_main, w_f, b_f, qn, kn, vmn)


def _cumsum_kernel(lf_ref, ft_ref):
    x = lf_ref[...]
    n = x.shape[0]
    row = lax.broadcasted_iota(jnp.int32, x.shape, 0)
    s = 1
    while s < n:
        x = x + jnp.where(row >= s, pltpu.roll(x, s, 0), 0.0)
        s *= 2
    ft_ref[...] = x.T[:N_HEADS, :] * LOG2E


def _cumsum(lf, batch, seq):
    return pl.pallas_call(
        _cumsum_kernel,
        grid=(batch,),
        in_specs=[pl.BlockSpec((seq, HEAD_DIM), lambda b: (b, 0))],
        out_specs=pl.BlockSpec((None, N_HEADS, seq), lambda b: (b, 0, 0)),
        out_shape=jax.ShapeDtypeStruct((batch, N_HEADS, seq), F32),
        compiler_params=_params("parallel"),
        name="cumsum",
    )(lf)


def _attn_kernel(q_ref, k_ref, v_ref, fr_ref, o_ref, m_ref, l_ref, acc_ref, *, t):
    i = pl.program_id(2)
    q = q_ref[...]

    def block(j):
        rows = pl.ds(pl.multiple_of(j * t, t), t)
        s = lax.dot_general(q, k_ref[rows, :], (((1,), (1,)), ((), ())),
                            preferred_element_type=F32)
        return s - fr_ref[pl.ds(j, 1), :], v_ref[rows, :]

    s, v = block(i)
    r = lax.broadcasted_iota(jnp.int32, (t, t), 0)
    c = lax.broadcasted_iota(jnp.int32, (t, t), 1)
    s = jnp.where(r >= c, s, -jnp.inf)
    m = jnp.max(s, axis=-1, keepdims=True)
    p = jnp.exp2(s - m)
    m_ref[...] = m
    l_ref[...] = jnp.sum(p, axis=-1, keepdims=True)
    acc_ref[...] = _dot(p.astype(BF16), v)

    def body(j, carry):
        s, v = block(j)
        m_old = m_ref[...]
        m_new = jnp.maximum(m_old, jnp.max(s, axis=-1, keepdims=True))
        alpha = jnp.exp2(m_old - m_new)
        p = jnp.exp2(s - m_new)
        m_ref[...] = m_new
        l_ref[...] = alpha * l_ref[...] + jnp.sum(p, axis=-1, keepdims=True)
        acc_ref[...] = alpha * acc_ref[...] + _dot(p.astype(BF16), v)
        return carry

    lax.fori_loop(0, i, body, 0)
    o_ref[...] = (acc_ref[...] / l_ref[...]).astype(o_ref.dtype)


def _attn(q, kb, vb, frow, batch, seq):
    t = ATT_TILE
    nq = seq // t
    frow = frow.reshape(batch, N_HEADS, nq, t)
    return pl.pallas_call(
        functools.partial(_attn_kernel, t=t),
        grid=(batch, N_HEADS, nq),
        in_specs=[
            pl.BlockSpec((t, HEAD_DIM), lambda b, h, i: (b * nq + i, h)),
            pl.BlockSpec((seq, HEAD_DIM), lambda b, h, i: (b, h)),
            pl.BlockSpec((seq, HEAD_DIM), lambda b, h, i: (b, h)),
            pl.BlockSpec((None, None, nq, t), lambda b, h, i: (b, h, 0, 0)),
        ],
        out_specs=pl.BlockSpec((t, HEAD_DIM), lambda b, h, i: (b * nq + i, h)),
        out_shape=jax.ShapeDtypeStruct((batch * seq, ATTN_WIDTH), BF16),
        scratch_shapes=[pltpu.VMEM((t, 1), F32), pltpu.VMEM((t, 1), F32),
                        pltpu.VMEM((t, HEAD_DIM), F32)],
        compiler_params=_params("parallel", "parallel", "arbitrary"),
        name="attn",
    )(q, kb, vb, frow)


def _mixout_kernel(att_ref, u_ref, vm_ref, sw_ref, sbt_ref, wo_ref, h_ref, o_ref, mlp_ref):
    tm = att_ref.shape[0]
    r = lax.broadcasted_iota(jnp.int32, (CHUNK, CHUNK), 0)
    c = lax.broadcasted_iota(jnp.int32, (CHUNK, CHUNK), 1)
    for g in range(N_GROUPS):
        w = jnp.where(r >= c, sw_ref[g], 0.0).astype(BF16)
        bias = sbt_ref[:, g:g + 1]
        cols = slice(g * HEAD_DIM, (g + 1) * HEAD_DIM)
        for n in range(tm // CHUNK):
            rows = slice(n * CHUNK, (n + 1) * CHUNK)
            mixed = _dot(w, vm_ref[rows, cols].astype(BF16)) + bias
            mlp_ref[rows, cols] = (u_ref[rows, cols] * mixed).astype(BF16)
    o_ref[...] = (h_ref[...] + _dot(att_ref[...], wo_ref[0:ATTN_WIDTH, :])
                  + _dot(mlp_ref[...], wo_ref[ATTN_WIDTH:, :]))


def _mixout(att, u, vm, sw, sbt, wo, h, tm):
    m, d = h.shape
    row = lambda w: pl.BlockSpec((tm, w), lambda i: (i, 0))
    return pl.pallas_call(
        _mixout_kernel,
        grid=(m // tm,),
        in_specs=[row(ATTN_WIDTH), row(MLP_WIDTH), row(MLP_WIDTH), _resident(sw.shape),
                  _resident(sbt.shape), _resident(wo.shape), row(d)],
        out_specs=row(d),
        out_shape=jax.ShapeDtypeStruct((m, d), F32),
        scratch_shapes=[pltpu.VMEM((tm, MLP_WIDTH), BF16)],
        compiler_params=_params("parallel"),
        name="mixout",
    )(att, u, vm, sw, sbt, wo, h)


def _mixout_one_kernel(att_ref, u_ref, vm_ref, w00_ref, b0_ref, wo_ref, h_ref, o_ref):
    mlp = u_ref[...] * (w00_ref[...] * vm_ref[...] + b0_ref[...])
    o_ref[...] = (h_ref[...] + _dot(att_ref[...].astype(BF16), wo_ref[0:ATTN_WIDTH, :])
                  + _dot(mlp.astype(BF16), wo_ref[ATTN_WIDTH:, :]))


def _mixout_one(att, u, vm, w00, b0, wo, h):
    m, d = h.shape
    full = lambda a: pl.BlockSpec(a.shape, lambda i: (0,) * a.ndim)
    args = (att, u, vm, w00, b0, wo, h)
    return pl.pallas_call(
        _mixout_one_kernel,
        grid=(1,),
        in_specs=[full(a) for a in args],
        out_specs=pl.BlockSpec((m, d), lambda i: (0, 0)),
        out_shape=jax.ShapeDtypeStruct((m, d), F32),
        compiler_params=_params("arbitrary"),
        name="mixout_one",
    )(*args)


def _decode_kernel(pt_ref, q_ref, kn_ref, vn_ref, lfn_ref, *refs, n_steps):
    g = DEC_PAGES
    ck, cv, clf = refs[0:g], refs[g:2 * g], refs[2 * g:3 * g]
    o_ref, m_ref, l_ref, acc_ref, c_ref = refs[3 * g:]
    step = pl.program_id(1)
    q = q_ref[...]
    sub = lax.broadcasted_iota(jnp.int32, (N_HEADS, HEAD_DIM), 0)
    lane = lax.broadcasted_iota(jnp.int32, (N_HEADS, PAGE), 1)

    def head_rows(page_ref, h):
        return page_ref[pl.ds(h, PAGE, stride=N_HEADS), :].astype(BF16)

    def own_head(per_head):
        out = per_head(0)
        for h in range(1, N_HEADS):
            out = jnp.where(sub == h, per_head(h), out)
        return out

    @pl.when(step == 0)
    def _():
        kn = kn_ref[...].astype(BF16).astype(F32)
        m_ref[...] = jnp.sum(q.astype(F32) * kn, axis=-1, keepdims=True)
        l_ref[...] = jnp.ones_like(l_ref)
        acc_ref[...] = vn_ref[...]
        c_ref[...] = lfn_ref[...]

    carry = c_ref[...]
    logits = []
    for j in range(g):
        lf = clf[j][...]
        y = lf
        s = 1
        while s < PAGE:
            y = y + jnp.where(lane + s < PAGE, pltpu.roll(y, PAGE - s, 1), 0.0)
            s *= 2
        decay = y - lf + carry
        carry = carry + y[:, 0:1]
        scores = own_head(lambda h: lax.dot_general(
            q, head_rows(ck[j], h), (((1,), (1,)), ((), ())),
            preferred_element_type=F32))
        logits.append(scores + decay)
    c_ref[...] = carry

    m_old = m_ref[...]
    m_new = m_old
    for lg in logits:
        m_new = jnp.maximum(m_new, jnp.max(lg, axis=-1, keepdims=True))
    alpha = jnp.exp(m_old - m_new)
    l = alpha * l_ref[...]
    acc = alpha * acc_ref[...]
    for j in range(g):
        pr = jnp.exp(logits[j] - m_new)
        l = l + jnp.sum(pr, axis=-1, keepdims=True)
        prb = pr.astype(BF16)
        acc = acc + own_head(lambda h: _dot(prb, head_rows(cv[j], h)))
    m_ref[...] = m_new
    l_ref[...] = l
    acc_ref[...] = acc

    @pl.when(step == n_steps - 1)
    def _():
        o_ref[...] = acc_ref[...] / l_ref[...]


def _decode(page_table, q, kn, vn, lfn, ck, cv, clft):
    nb, n_pages = page_table.shape
    g = DEC_PAGES
    n_steps = n_pages // g
    per_b = lambda b, s, pt: (b, 0, 0)

    def page(j, ndim):
        return lambda b, s, pt: (pt[b, n_pages - 1 - (s * g + j)],) + (0,) * (ndim - 1)

    kv_spec = lambda j: pl.BlockSpec((None, PAGE * N_HEADS, HEAD_DIM), page(j, 3))
    lf_spec = lambda j: pl.BlockSpec((None, N_HEADS, PAGE), page(j, 3))
    head_block = pl.BlockSpec((None, N_HEADS, HEAD_DIM), per_b)
    grid_spec = pltpu.PrefetchScalarGridSpec(
        num_scalar_prefetch=1,
        grid=(nb, n_steps),
        in_specs=([head_block, head_block, head_block, pl.BlockSpec((None, N_HEADS, 1), per_b)]
                  + [kv_spec(j) for j in range(g)] + [kv_spec(j) for j in range(g)]
                  + [lf_spec(j) for j in range(g)]),
        out_specs=head_block,
        scratch_shapes=[pltpu.VMEM((N_HEADS, 1), F32), pltpu.VMEM((N_HEADS, 1), F32),
                        pltpu.VMEM((N_HEADS, HEAD_DIM), F32), pltpu.VMEM((N_HEADS, 1), F32)],
    )
    return pl.pallas_call(
        functools.partial(_decode_kernel, n_steps=n_steps),
        grid_spec=grid_spec,
        out_shape=jax.ShapeDtypeStruct((nb, N_HEADS, HEAD_DIM), F32),
        compiler_params=_params("parallel", "arbitrary"),
        name="decode",
    )(page_table, q, kn, vn, lfn, *([ck] * g), *([cv] * g), *([clft] * g))


def _ple_kernel(h_ref, p_ref, nw_ref, wg_ref, wp_ref, o_ref):
    h = h_ref[...]
    gate = jax.nn.sigmoid(_dot(_rms(h, nw_ref[...]).astype(BF16), wg_ref[...]))
    o_ref[...] = h + gate * _dot(p_ref[...].astype(BF16), wp_ref[...])


def _ple(h, p, nw, wg, wp, tm):
    m, d = h.shape
    return pl.pallas_call(
        _ple_kernel,
        grid=(m // tm,),
        in_specs=[pl.BlockSpec((tm, d), lambda i: (i, 0)),
                  pl.BlockSpec((tm, p.shape[1]), lambda i: (i, 0)),
                  pl.BlockSpec((1, d), lambda i: (0, 0)),
                  _resident(wg.shape), _resident(wp.shape)],
        out_specs=pl.BlockSpec((tm, d), lambda i: (i, 0)),
        out_shape=jax.ShapeDtypeStruct((m, d), F32),
        compiler_params=_params("parallel"),
        name="ple",
    )(h, p, nw, wg, wp)


def _pad_cols(w, mult):
    return jnp.pad(w, ((0, 0), (0, (-w.shape[1]) % mult)))


def kernel(x_prompt, x_sample, p_prompt, p_sample, cache_k, cache_v, cache_logf, page_table,
           ffn1_norm, ffn1_w_gate, ffn1_w_up, ffn1_w_down, mix_norm, w_in, b_f, q_norm, k_norm,
           v_mlp_norm, spatial_w, spatial_b, w_o, ffn2_norm, ffn2_w_gate, ffn2_w_up, ffn2_w_down,
           ple_norm, w_ple_gate, w_ple_proj):
    batch, seq, d = x_prompt.shape
    nb, dec_seq, _ = x_sample.shape
    depth = ffn1_norm.shape[0]
    m = batch * seq
    assert depth == 1 and dec_seq == 1
    assert seq % ATT_TILE == 0 and m % ROW_TILE == 0 and m % PROJ_TILE == 0
    assert page_table.shape[1] % DEC_PAGES == 0
    ffn_tm = FFN_ROW_TILE if m % FFN_ROW_TILE == 0 else ROW_TILE
    l = 0

    def ffn_weights(wg, wu, wd):
        wd = jnp.pad(wd, ((0, (-wd.shape[0]) % FF_TILE), (0, 0)))
        return (_pad_cols(wg, FF_TILE).astype(BF16), _pad_cols(wu, FF_TILE).astype(BF16),
                wd.astype(BF16))

    ffn1 = (ffn1_norm[l][None],) + ffn_weights(ffn1_w_gate[l], ffn1_w_up[l], ffn1_w_down[l])
    ffn2 = (ffn2_norm[l][None],) + ffn_weights(ffn2_w_gate[l], ffn2_w_up[l], ffn2_w_down[l])
    n_main = 3 * ATTN_WIDTH + 2 * MLP_WIDTH
    proj = (mix_norm[l][None], w_in[l][:, :n_main].astype(BF16),
            _pad_cols(w_in[l][:, n_main:], HEAD_DIM).astype(BF16),
            _pad_cols(b_f[l][None], HEAD_DIM), q_norm[l][None], k_norm[l][None],
            v_mlp_norm[l].reshape(1, MLP_WIDTH))
    wo = w_o[l].astype(BF16)
    ple = (ple_norm[l][None], w_ple_gate[l].astype(BF16), w_ple_proj[l].astype(BF16))
    heads = (N_HEADS, HEAD_DIM)

    hp = _ffn(x_prompt.reshape(m, d), *ffn1, tm=ffn_tm)
    q, k, v, kb, vb, lf, u, vm = _proj(hp, *proj, tm=PROJ_TILE, q_scale=QK_SCALE * LOG2E)
    frow = _cumsum(lf, batch, seq)
    att = _attn(q, kb, vb, frow, batch, seq)
    hp = _mixout(att, u, vm, spatial_w[l], spatial_b[l].T, wo, hp, tm=ROW_TILE)
    hp = _ffn(hp, *ffn2, tm=ffn_tm)
    hp = _ple(hp, p_prompt[l].reshape(m, -1), *ple, tm=ROW_TILE)

    hs = _ffn(x_sample.reshape(nb, d), *ffn1, tm=nb)
    qs, ks, vs, _, _, lfs, us, vms = _proj(hs, *proj, tm=nb, q_scale=QK_SCALE)
    att_s = _decode(page_table, qs.reshape(nb, *heads), ks.reshape(nb, *heads),
                    vs.reshape(nb, *heads), lfs[:, :N_HEADS].reshape(nb, N_HEADS, 1),
                    cache_k[l].reshape(-1, PAGE * N_HEADS, HEAD_DIM),
                    cache_v[l].reshape(-1, PAGE * N_HEADS, HEAD_DIM),
                    jnp.swapaxes(cache_logf[l], 1, 2))
    w00 = jnp.repeat(spatial_w[l][:, 0, 0], HEAD_DIM)[None]
    b0 = jnp.repeat(spatial_b[l][:, 0], HEAD_DIM)[None]
    hs = _mixout_one(att_s.reshape(nb, ATTN_WIDTH), us, vms, w00, b0, wo, hs)
    hs = _ffn(hs, *ffn2, tm=nb)
    hs = _ple(hs, p_sample[l].reshape(nb, -1), *ple, tm=nb)

    return (hp.reshape(batch, seq, d), hs.reshape(nb, 1, d),
            k.reshape(1, batch, seq, *heads), v.reshape(1, batch, seq, *heads),
            lf[:, :N_HEADS].reshape(1, batch, seq, N_HEADS),
            ks.reshape(1, nb, 1, *heads), vs.reshape(1, nb, 1, *heads),
            lfs[:, :N_HEADS].reshape(1, nb, 1, N_HEADS),
            vms.reshape(1, nb, 1, N_GROUPS, HEAD_DIM))
```

```python
import functools

import jax
import jax.numpy as jnp
from jax import lax
from jax.experimental import pallas as pl
from jax.experimental.pallas import tpu as pltpu

F32 = jnp.float32
BF16 = jnp.bfloat16

HEAD_DIM = 128
N_HEADS = 8
N_GROUPS = 8
ATTN_WIDTH = N_HEADS * HEAD_DIM
MLP_WIDTH = N_GROUPS * HEAD_DIM
CHUNK = 128
PAGE = 128
RMS_EPS = 1e-6
QK_SCALE = HEAD_DIM ** -0.5
LOG2E = 1.4426950408889634

V7X_VMEM_BYTES = 64 * 1024 * 1024
VMEM_LIMIT = V7X_VMEM_BYTES - 8 * 1024 * 1024

FF_TILE = 512
FFN_ROW_TILE = 1024
ROW_TILE = 512
PROJ_TILE = 256
ATT_TILE = 512
DEC_PAGES = 16


def _params(*sem):
    return pltpu.CompilerParams(dimension_semantics=sem, vmem_limit_bytes=VMEM_LIMIT)


def _resident(shape):
    zeros = (0,) * len(shape)
    return pl.BlockSpec(shape, lambda *_: zeros, pipeline_mode=pl.Buffered(1))


def _rms(x, w):
    return x * lax.rsqrt(jnp.mean(x * x, axis=-1, keepdims=True) + RMS_EPS) * w


def _dot(a, b):
    return jnp.dot(a, b, preferred_element_type=F32)


def _cast_kernel(w_ref, o_ref, *, axis, size, tile):
    pos = pl.program_id(0) * tile + lax.broadcasted_iota(jnp.int32, w_ref.shape, axis)
    o_ref[...] = jnp.where(pos < size, w_ref[...], 0.0).astype(BF16)


def _cast_padded(w, axis):
    size = w.shape[axis]
    n = pl.cdiv(size, FF_TILE)
    block = tuple(FF_TILE if a == axis else s for a, s in enumerate(w.shape))
    index = (lambda i: (i, 0)) if axis == 0 else (lambda i: (0, i))
    out_shape = tuple(n * FF_TILE if a == axis else s for a, s in enumerate(w.shape))
    return pl.pallas_call(
        functools.partial(_cast_kernel, axis=axis, size=size, tile=FF_TILE),
        grid=(n,),
        in_specs=[pl.BlockSpec(block, index)],
        out_specs=pl.BlockSpec(block, index),
        out_shape=jax.ShapeDtypeStruct(out_shape, BF16),
        compiler_params=_params("parallel"),
        name="cast",
    )(w)


def _ffn_kernel(x_ref, nw_ref, wg_ref, wu_ref, wd_ref, o_ref, xn_ref, *, nf):
    f = pl.program_id(1)

    @pl.when(f == 0)
    def _():
        xn_ref[...] = _rms(x_ref[...], nw_ref[...]).astype(BF16)
        o_ref[...] = jnp.zeros_like(o_ref)

    xn = xn_ref[...]
    g = _dot(xn, wg_ref[...])
    u = _dot(xn, wu_ref[...])
    h = (jax.nn.silu(g) * u).astype(BF16)
    o_ref[...] += _dot(h, wd_ref[...])

    @pl.when(f == nf - 1)
    def _():
        o_ref[...] = x_ref[...] + 0.5 * o_ref[...]


def _ffn(x, nw, wg, wu, wd, tm):
    m, d = x.shape
    nf = wg.shape[1] // FF_TILE
    return pl.pallas_call(
        functools.partial(_ffn_kernel, nf=nf),
        grid=(m // tm, nf),
        in_specs=[
            pl.BlockSpec((tm, d), lambda i, f: (i, 0)),
            pl.BlockSpec((1, d), lambda i, f: (0, 0)),
            pl.BlockSpec((d, FF_TILE), lambda i, f: (0, f)),
            pl.BlockSpec((d, FF_TILE), lambda i, f: (0, f)),
            pl.BlockSpec((FF_TILE, d), lambda i, f: (f, 0)),
        ],
        out_specs=pl.BlockSpec((tm, d), lambda i, f: (i, 0)),
        out_shape=jax.ShapeDtypeStruct((m, d), F32),
        scratch_shapes=[pltpu.VMEM((tm, d), BF16)],
        compiler_params=_params("parallel", "arbitrary"),
        name="ffn",
    )(x, nw, wg, wu, wd)


def _group_rms(z, gain):
    outs = []
    for h in range(z.shape[1] // HEAD_DIM):
        sl = slice(h * HEAD_DIM, (h + 1) * HEAD_DIM)
        g = gain if gain.shape[1] == HEAD_DIM else gain[:, sl]
        outs.append(_rms(z[:, sl], g))
    return outs


def _proj_kernel(h_ref, nw_ref, w_ref, wf_ref, bf_ref, qn_ref, kn_ref, vmn_ref,
                 q_ref, k_ref, v_ref, kb_ref, vb_ref, lf_ref, u_ref, vm_ref, *, q_scale):
    a = _rms(h_ref[...], nw_ref[...]).astype(BF16)
    aw = ATTN_WIDTH

    zq = _dot(a, w_ref[:, 0:aw])
    for h, qh in enumerate(_group_rms(zq, qn_ref[...])):
        q_ref[:, h * HEAD_DIM:(h + 1) * HEAD_DIM] = (qh * q_scale).astype(BF16)

    zk = _dot(a, w_ref[:, aw:2 * aw])
    for h, kh in enumerate(_group_rms(zk, kn_ref[...])):
        sl = slice(h * HEAD_DIM, (h + 1) * HEAD_DIM)
        k_ref[:, sl] = kh
        kb_ref[:, sl] = kh.astype(BF16)

    zv = _dot(a, w_ref[:, 2 * aw:3 * aw])
    v_ref[...] = zv
    vb_ref[...] = zv.astype(BF16)

    zu = _dot(a, w_ref[:, 3 * aw:3 * aw + MLP_WIDTH])
    u_ref[...] = jax.nn.gelu(zu)

    zm = _dot(a, w_ref[:, 3 * aw + MLP_WIDTH:3 * aw + 2 * MLP_WIDTH])
    for g, vg in enumerate(_group_rms(jax.nn.gelu(zm), vmn_ref[...])):
        vm_ref[:, g * HEAD_DIM:(g + 1) * HEAD_DIM] = vg

    zf = _dot(a, wf_ref[...]) + bf_ref[...]
    lf_ref[...] = jnp.minimum(zf, 0.0) - jnp.log1p(jnp.exp(-jnp.abs(zf)))


def _proj(h, nw, w_main, w_f, b_f, qn, kn, vmn, tm, q_scale):
    m, d = h.shape
    row = lambda w: pl.BlockSpec((tm, w), lambda i: (i, 0))
    small = lambda w: pl.BlockSpec((1, w), lambda i: (0, 0))
    out_w = [(ATTN_WIDTH, BF16), (ATTN_WIDTH, F32), (ATTN_WIDTH, F32), (ATTN_WIDTH, BF16),
             (ATTN_WIDTH, BF16), (HEAD_DIM, F32), (MLP_WIDTH, F32), (MLP_WIDTH, F32)]
    return pl.pallas_call(
        functools.partial(_proj_kernel, q_scale=q_scale),
        grid=(m // tm,),
        in_specs=[row(d), small(d), _resident(w_main.shape), _resident(w_f.shape),
                  small(HEAD_DIM), small(HEAD_DIM), small(HEAD_DIM), small(MLP_WIDTH)],
        out_specs=[row(w) for w, _ in out_w],
        out_shape=[jax.ShapeDtypeStruct((m, w), dt) for w, dt in out_w],
        compiler_params=_params("parallel"),
        name="proj",
    )(h, nw, w_main, w_f, b_f, qn, kn, vmn)


def _cumsum_kernel(lf_ref, ft_ref):
    x = lf_ref[...]
    n = x.shape[0]
    row = lax.broadcasted_iota(jnp.int32, x.shape, 0)
    s = 1
    while s < n:
        x = x + jnp.where(row >= s, pltpu.roll(x, s, 0), 0.0)
        s *= 2
    ft_ref[...] = x.T[:N_HEADS, :] * LOG2E


def _cumsum(lf, batch, seq):
    return pl.pallas_call(
        _cumsum_kernel,
        grid=(batch,),
        in_specs=[pl.BlockSpec((seq, HEAD_DIM), lambda b: (b, 0))],
        out_specs=pl.BlockSpec((None, N_HEADS, seq), lambda b: (b, 0, 0)),
        out_shape=jax.ShapeDtypeStruct((batch, N_HEADS, seq), F32),
        compiler_params=_params("parallel"),
        name="cumsum",
    )(lf)


def _attn_kernel(q_ref, k_ref, v_ref, fr_ref, o_ref, *, t, nq):
    i = pl.program_id(2)
    q = q_ref[...]
    r = lax.broadcasted_iota(jnp.int32, (t, t), 0)
    c = lax.broadcasted_iota(jnp.int32, (t, t), 1)
    nt = (((1,), (1,)), ((), ()))

    def query_block(n):
        past = n - t
        s_diag = (lax.dot_general(q, k_ref[past:n, :], nt, preferred_element_type=F32)
                  - fr_ref[:, past:n])
        s_diag = jnp.where(r >= c, s_diag, -jnp.inf)
        m = jnp.max(s_diag, axis=-1, keepdims=True)
        if past:
            s_past = (lax.dot_general(q, k_ref[0:past, :], nt, preferred_element_type=F32)
                      - fr_ref[:, 0:past])
            m = jnp.maximum(m, jnp.max(s_past, axis=-1, keepdims=True))
        p = jnp.exp2(s_diag - m)
        l = jnp.sum(p, axis=-1, keepdims=True)
        acc = _dot(p.astype(BF16), v_ref[past:n, :])
        if past:
            p = jnp.exp2(s_past - m)
            l = l + jnp.sum(p, axis=-1, keepdims=True)
            acc = acc + _dot(p.astype(BF16), v_ref[0:past, :])
        o_ref[...] = (acc / l).astype(o_ref.dtype)

    for blk in range(nq):
        pl.when(i == blk)(functools.partial(query_block, (blk + 1) * t))


def _attn(q, kb, vb, frow, batch, seq):
    t = ATT_TILE
    nq = seq // t
    frow = frow.reshape(batch, N_HEADS, 1, seq)
    return pl.pallas_call(
        functools.partial(_attn_kernel, t=t, nq=nq),
        grid=(batch, N_HEADS, nq),
        in_specs=[
            pl.BlockSpec((t, HEAD_DIM), lambda b, h, i: (b * nq + i, h)),
            pl.BlockSpec((seq, HEAD_DIM), lambda b, h, i: (b, h)),
            pl.BlockSpec((seq, HEAD_DIM), lambda b, h, i: (b, h)),
            pl.BlockSpec((None, None, 1, seq), lambda b, h, i: (b, h, 0, 0)),
        ],
        out_specs=pl.BlockSpec((t, HEAD_DIM), lambda b, h, i: (b * nq + i, h)),
        out_shape=jax.ShapeDtypeStruct((batch * seq, ATTN_WIDTH), BF16),
        compiler_params=_params("parallel", "parallel", "arbitrary"),
        name="attn",
    )(q, kb, vb, frow)


def _mixout_kernel(att_ref, u_ref, vm_ref, sw_ref, sbt_ref, wo_ref, h_ref, o_ref, mlp_ref):
    tm = att_ref.shape[0]
    r = lax.broadcasted_iota(jnp.int32, (CHUNK, CHUNK), 0)
    c = lax.broadcasted_iota(jnp.int32, (CHUNK, CHUNK), 1)
    for g in range(N_GROUPS):
        w = jnp.where(r >= c, sw_ref[g], 0.0).astype(BF16)
        bias = sbt_ref[:, g:g + 1]
        cols = slice(g * HEAD_DIM, (g + 1) * HEAD_DIM)
        for n in range(tm // CHUNK):
            rows = slice(n * CHUNK, (n + 1) * CHUNK)
            mixed = _dot(w, vm_ref[rows, cols].astype(BF16)) + bias
            mlp_ref[rows, cols] = (u_ref[rows, cols] * mixed).astype(BF16)
    o_ref[...] = (h_ref[...] + _dot(att_ref[...], wo_ref[0:ATTN_WIDTH, :])
                  + _dot(mlp_ref[...], wo_ref[ATTN_WIDTH:, :]))


def _mixout(att, u, vm, sw, sbt, wo, h, tm):
    m, d = h.shape
    row = lambda w: pl.BlockSpec((tm, w), lambda i: (i, 0))
    return pl.pallas_call(
        _mixout_kernel,
        grid=(m // tm,),
        in_specs=[row(ATTN_WIDTH), row(MLP_WIDTH), row(MLP_WIDTH), _resident(sw.shape),
                  _resident(sbt.shape), _resident(wo.shape), row(d)],
        out_specs=row(d),
        out_shape=jax.ShapeDtypeStruct((m, d), F32),
        scratch_shapes=[pltpu.VMEM((tm, MLP_WIDTH), BF16)],
        compiler_params=_params("parallel"),
        name="mixout",
    )(att, u, vm, sw, sbt, wo, h)


def _mixout_one_kernel(att_ref, u_ref, vm_ref, w00_ref, b0_ref, wo_ref, h_ref, o_ref):
    mlp = u_ref[...] * (w00_ref[...] * vm_ref[...] + b0_ref[...])
    o_ref[...] = (h_ref[...] + _dot(att_ref[...].astype(BF16), wo_ref[0:ATTN_WIDTH, :])
                  + _dot(mlp.astype(BF16), wo_ref[ATTN_WIDTH:, :]))


def _mixout_one(att, u, vm, w00, b0, wo, h):
    m, d = h.shape
    full = lambda a: pl.BlockSpec(a.shape, lambda i: (0,) * a.ndim)
    args = (att, u, vm, w00, b0, wo, h)
    return pl.pallas_call(
        _mixout_one_kernel,
        grid=(1,),
        in_specs=[full(a) for a in args],
        out_specs=pl.BlockSpec((m, d), lambda i: (0, 0)),
        out_shape=jax.ShapeDtypeStruct((m, d), F32),
        compiler_params=_params("arbitrary"),
        name="mixout_one",
    )(*args)


def _decode_pages(q, state, ck, cv, clf):
    m_old, l_old, acc_old, carry = state
    sub = lax.broadcasted_iota(jnp.int32, (N_HEADS, HEAD_DIM), 0)
    lane = lax.broadcasted_iota(jnp.int32, (N_HEADS, PAGE), 1)

    def head_rows(page_ref, h):
        return page_ref[pl.ds(h, PAGE, stride=N_HEADS), :].astype(BF16)

    def own_head(per_head):
        out = per_head(0)
        for h in range(1, N_HEADS):
            out = jnp.where(sub == h, per_head(h), out)
        return out

    logits = []
    for j in range(len(ck)):
        lf = clf[j][...]
        y = lf
        s = 1
        while s < PAGE:
            y = y + jnp.where(lane + s < PAGE, pltpu.roll(y, PAGE - s, 1), 0.0)
            s *= 2
        decay = y - lf + carry
        carry = carry + y[:, 0:1]
        scores = own_head(lambda h: lax.dot_general(
            q, head_rows(ck[j], h), (((1,), (1,)), ((), ())),
            preferred_element_type=F32))
        logits.append(scores + decay)

    m_new = m_old
    for lg in logits:
        m_new = jnp.maximum(m_new, jnp.max(lg, axis=-1, keepdims=True))
    alpha = jnp.exp(m_old - m_new)
    l = alpha * l_old
    acc = alpha * acc_old
    for j in range(len(cv)):
        pr = jnp.exp(logits[j] - m_new)
        l = l + jnp.sum(pr, axis=-1, keepdims=True)
        prb = pr.astype(BF16)
        acc = acc + own_head(lambda h: _dot(prb, head_rows(cv[j], h)))
    return m_new, l, acc, carry


def _decode_kernel(pt_ref, q_ref, kn_ref, vn_ref, lfn_ref, *refs, n_steps):
    g = DEC_PAGES
    ck, cv, clf = refs[0:g], refs[g:2 * g], refs[2 * g:3 * g]
    o_ref, m_ref, l_ref, acc_ref, c_ref = refs[3 * g:]
    step = pl.program_id(1)
    q = q_ref[...]

    @pl.when(step == 0)
    def _():
        kn = kn_ref[...].astype(BF16).astype(F32)
        m_ref[...] = jnp.sum(q.astype(F32) * kn, axis=-1, keepdims=True)
        l_ref[...] = jnp.ones_like(l_ref)
        acc_ref[...] = vn_ref[...]
        c_ref[...] = lfn_ref[...]

    state = (m_ref[...], l_ref[...], acc_ref[...], c_ref[...])
    m_ref[...], l_ref[...], acc_ref[...], c_ref[...] = _decode_pages(q, state, ck, cv, clf)

    @pl.when(step == n_steps - 1)
    def _():
        o_ref[...] = acc_ref[...] / l_ref[...]


def _decode(page_table, q, kn, vn, lfn, ck, cv, clft):
    nb, n_pages = page_table.shape
    g = DEC_PAGES
    n_steps = n_pages // g
    per_b = lambda b, s, pt: (b, 0, 0)

    def page(j):
        return lambda b, s, pt: (pt[b, n_pages - 1 - (s * g + j)], 0, 0)

    kv_spec = lambda j: pl.BlockSpec((None, PAGE * N_HEADS, HEAD_DIM), page(j))
    lf_spec = lambda j: pl.BlockSpec((None, N_HEADS, PAGE), page(j))
    head_block = pl.BlockSpec((None, N_HEADS, HEAD_DIM), per_b)
    grid_spec = pltpu.PrefetchScalarGridSpec(
        num_scalar_prefetch=1,
        grid=(nb, n_steps),
        in_specs=([head_block, head_block, head_block, pl.BlockSpec((None, N_HEADS, 1), per_b)]
                  + [kv_spec(j) for j in range(g)] + [kv_spec(j) for j in range(g)]
                  + [lf_spec(j) for j in range(g)]),
        out_specs=head_block,
        scratch_shapes=[pltpu.VMEM((N_HEADS, 1), F32), pltpu.VMEM((N_HEADS, 1), F32),
                        pltpu.VMEM((N_HEADS, HEAD_DIM), F32), pltpu.VMEM((N_HEADS, 1), F32)],
    )
    return pl.pallas_call(
        functools.partial(_decode_kernel, n_steps=n_steps),
        grid_spec=grid_spec,
        out_shape=jax.ShapeDtypeStruct((nb, N_HEADS, HEAD_DIM), F32),
        compiler_params=_params("parallel", "arbitrary"),
        name="decode",
    )(page_table, q, kn, vn, lfn, *([ck] * g), *([cv] * g), *([clft] * g))


def _ple_kernel(h_ref, p_ref, nw_ref, wg_ref, wp_ref, o_ref):
    h = h_ref[...]
    gate = jax.nn.sigmoid(_dot(_rms(h, nw_ref[...]).astype(BF16), wg_ref[...]))
    o_ref[...] = h + gate * _dot(p_ref[...].astype(BF16), wp_ref[...])


def _ple(h, p, nw, wg, wp, tm):
    m, d = h.shape
    return pl.pallas_call(
        _ple_kernel,
        grid=(m // tm,),
        in_specs=[pl.BlockSpec((tm, d), lambda i: (i, 0)),
                  pl.BlockSpec((tm, p.shape[1]), lambda i: (i, 0)),
                  pl.BlockSpec((1, d), lambda i: (0, 0)),
                  _resident(wg.shape), _resident(wp.shape)],
        out_specs=pl.BlockSpec((tm, d), lambda i: (i, 0)),
        out_shape=jax.ShapeDtypeStruct((m, d), F32),
        compiler_params=_params("parallel"),
        name="ple",
    )(h, p, nw, wg, wp)


def _pad_cols(w, mult):
    return jnp.pad(w, ((0, 0), (0, (-w.shape[1]) % mult)))


def kernel(x_prompt, x_sample, p_prompt, p_sample, cache_k, cache_v, cache_logf, page_table,
           ffn1_norm, ffn1_w_gate, ffn1_w_up, ffn1_w_down, mix_norm, w_in, b_f, q_norm, k_norm,
           v_mlp_norm, spatial_w, spatial_b, w_o, ffn2_norm, ffn2_w_gate, ffn2_w_up, ffn2_w_down,
           ple_norm, w_ple_gate, w_ple_proj):
    batch, seq, d = x_prompt.shape
    nb, dec_seq, _ = x_sample.shape
    depth = ffn1_norm.shape[0]
    m = batch * seq
    assert depth == 1 and dec_seq == 1
    assert seq % ATT_TILE == 0 and m % ROW_TILE == 0 and m % PROJ_TILE == 0
    assert page_table.shape[1] % DEC_PAGES == 0
    ffn_tm = FFN_ROW_TILE if m % FFN_ROW_TILE == 0 else ROW_TILE
    l = 0

    def ffn_weights(wg, wu, wd):
        return _cast_padded(wg, 1), _cast_padded(wu, 1), _cast_padded(wd, 0)

    ffn1 = (ffn1_norm[l][None],) + ffn_weights(ffn1_w_gate[l], ffn1_w_up[l], ffn1_w_down[l])
    ffn2 = (ffn2_norm[l][None],) + ffn_weights(ffn2_w_gate[l], ffn2_w_up[l], ffn2_w_down[l])
    n_main = 3 * ATTN_WIDTH + 2 * MLP_WIDTH
    proj = (mix_norm[l][None], w_in[l][:, :n_main].astype(BF16),
            _pad_cols(w_in[l][:, n_main:], HEAD_DIM).astype(BF16),
            _pad_cols(b_f[l][None], HEAD_DIM), q_norm[l][None], k_norm[l][None],
            v_mlp_norm[l].reshape(1, MLP_WIDTH))
    wo = w_o[l].astype(BF16)
    ple = (ple_norm[l][None], w_ple_gate[l].astype(BF16), w_ple_proj[l].astype(BF16))
    heads = (N_HEADS, HEAD_DIM)

    hp = _ffn(x_prompt.reshape(m, d), *ffn1, tm=ffn_tm)
    q, k, v, kb, vb, lf, u, vm = _proj(hp, *proj, tm=PROJ_TILE, q_scale=QK_SCALE * LOG2E)
    frow = _cumsum(lf, batch, seq)
    att = _attn(q, kb, vb, frow, batch, seq)
    hp = _mixout(att, u, vm, spatial_w[l], spatial_b[l].T, wo, hp, tm=ROW_TILE)
    hp = _ffn(hp, *ffn2, tm=ffn_tm)
    hp = _ple(hp, p_prompt[l].reshape(m, -1), *ple, tm=ROW_TILE)

    hs = _ffn(x_sample.reshape(nb, d), *ffn1, tm=nb)
    qs, ks, vs, _, _, lfs, us, vms = _proj(hs, *proj, tm=nb, q_scale=QK_SCALE)
    att_s = _decode(page_table, qs.reshape(nb, *heads), ks.reshape(nb, *heads),
                    vs.reshape(nb, *heads), lfs[:, :N_HEADS].reshape(nb, N_HEADS, 1),
                    cache_k[l].reshape(-1, PAGE * N_HEADS, HEAD_DIM),
                    cache_v[l].reshape(-1, PAGE * N_HEADS, HEAD_DIM),
                    jnp.swapaxes(cache_logf[l], 1, 2))
    w00 = jnp.repeat(spatial_w[l][:, 0, 0], HEAD_DIM)[None]
    b0 = jnp.repeat(spatial_b[l][:, 0], HEAD_DIM)[None]
    hs = _mixout_one(att_s.reshape(nb, ATTN_WIDTH), us, vms, w00, b0, wo, hs)
    hs = _ffn(hs, *ffn2, tm=nb)
    hs = _ple(hs, p_sample[l].reshape(nb, -1), *ple, tm=nb)

    return (hp.reshape(batch, seq, d), hs.reshape(nb, 1, d),
            k.reshape(1, batch, seq, *heads), v.reshape(1, batch, seq, *heads),
            lf[:, :N_HEADS].reshape(1, batch, seq, N_HEADS),
            ks.reshape(1, nb, 1, *heads), vs.reshape(1, nb, 1, *heads),
            lfs[:, :N_HEADS].reshape(1, nb, 1, N_HEADS),
            vms.reshape(1, nb, 1, N_GROUPS, HEAD_DIM))
```

```python
import functools

import jax
import jax.numpy as jnp
from jax import lax
from jax.experimental import pallas as pl
from jax.experimental.pallas import tpu as pltpu

F32 = jnp.float32
BF16 = jnp.bfloat16

HEAD_DIM = 128
N_HEADS = 8
N_GROUPS = 8
ATTN_WIDTH = N_HEADS * HEAD_DIM
MLP_WIDTH = N_GROUPS * HEAD_DIM
CHUNK = 128
PAGE = 128
RMS_EPS = 1e-6
QK_SCALE = HEAD_DIM ** -0.5
LOG2E = 1.4426950408889634

V7X_VMEM_BYTES = 64 * 1024 * 1024
VMEM_LIMIT = V7X_VMEM_BYTES - 8 * 1024 * 1024

FF_TILE = 512
FFN_ROW_TILE = 1024
ROW_TILE = 512
PROJ_TILE = 256
ATT_TILE = 512
DEC_PAGES = 16


def _params(*sem):
    return pltpu.CompilerParams(dimension_semantics=sem, vmem_limit_bytes=VMEM_LIMIT)


def _resident(shape):
    zeros = (0,) * len(shape)
    return pl.BlockSpec(shape, lambda *_: zeros, pipeline_mode=pl.Buffered(1))


def _rms(x, w):
    return x * lax.rsqrt(jnp.mean(x * x, axis=-1, keepdims=True) + RMS_EPS) * w


def _dot(a, b):
    return jnp.dot(a, b, preferred_element_type=F32)


def _cast_kernel(w_ref, o_ref, *, axis, size, tile):
    pos = pl.program_id(0) * tile + lax.broadcasted_iota(jnp.int32, w_ref.shape, axis)
    o_ref[...] = jnp.where(pos < size, w_ref[...], 0.0).astype(BF16)


def _cast_padded(w, axis):
    size = w.shape[axis]
    n = pl.cdiv(size, FF_TILE)
    block = tuple(FF_TILE if a == axis else s for a, s in enumerate(w.shape))
    index = (lambda i: (i, 0)) if axis == 0 else (lambda i: (0, i))
    out_shape = tuple(n * FF_TILE if a == axis else s for a, s in enumerate(w.shape))
    return pl.pallas_call(
        functools.partial(_cast_kernel, axis=axis, size=size, tile=FF_TILE),
        grid=(n,),
        in_specs=[pl.BlockSpec(block, index)],
        out_specs=pl.BlockSpec(block, index),
        out_shape=jax.ShapeDtypeStruct(out_shape, BF16),
        compiler_params=_params("parallel"),
        name="cast",
    )(w)


def _ffn_kernel(x_ref, nw_ref, wg_ref, wu_ref, wd_ref, o_ref, xn_ref, *, nf):
    f = pl.program_id(1)

    @pl.when(f == 0)
    def _():
        xn_ref[...] = _rms(x_ref[...], nw_ref[...]).astype(BF16)
        o_ref[...] = jnp.zeros_like(o_ref)

    xn = xn_ref[...]
    g = _dot(xn, wg_ref[...])
    u = _dot(xn, wu_ref[...])
    h = (jax.nn.silu(g) * u).astype(BF16)
    o_ref[...] += _dot(h, wd_ref[...])

    @pl.when(f == nf - 1)
    def _():
        o_ref[...] = x_ref[...] + 0.5 * o_ref[...]


def _ffn(x, nw, wg, wu, wd, tm):
    m, d = x.shape
    nf = wg.shape[1] // FF_TILE
    return pl.pallas_call(
        functools.partial(_ffn_kernel, nf=nf),
        grid=(m // tm, nf),
        in_specs=[
            pl.BlockSpec((tm, d), lambda i, f: (i, 0)),
            pl.BlockSpec((1, d), lambda i, f: (0, 0)),
            pl.BlockSpec((d, FF_TILE), lambda i, f: (0, f)),
            pl.BlockSpec((d, FF_TILE), lambda i, f: (0, f)),
            pl.BlockSpec((FF_TILE, d), lambda i, f: (f, 0)),
        ],
        out_specs=pl.BlockSpec((tm, d), lambda i, f: (i, 0)),
        out_shape=jax.ShapeDtypeStruct((m, d), F32),
        scratch_shapes=[pltpu.VMEM((tm, d), BF16)],
        compiler_params=_params("parallel", "arbitrary"),
        name="ffn",
    )(x, nw, wg, wu, wd)


def _group_rms(z, gain):
    outs = []
    for h in range(z.shape[1] // HEAD_DIM):
        sl = slice(h * HEAD_DIM, (h + 1) * HEAD_DIM)
        g = gain if gain.shape[1] == HEAD_DIM else gain[:, sl]
        outs.append(_rms(z[:, sl], g))
    return outs


def _proj_kernel(h_ref, nw_ref, w_ref, wf_ref, bf_ref, qn_ref, kn_ref, vmn_ref,
                 q_ref, k_ref, v_ref, lf_ref, u_ref, vm_ref, *attn_refs, q_scale):
    a = _rms(h_ref[...], nw_ref[...]).astype(BF16)
    aw = ATTN_WIDTH

    zq = _dot(a, w_ref[:, 0:aw])
    for h, qh in enumerate(_group_rms(zq, qn_ref[...])):
        q_ref[:, h * HEAD_DIM:(h + 1) * HEAD_DIM] = (qh * q_scale).astype(BF16)

    zk = _dot(a, w_ref[:, aw:2 * aw])
    for h, kh in enumerate(_group_rms(zk, kn_ref[...])):
        sl = slice(h * HEAD_DIM, (h + 1) * HEAD_DIM)
        k_ref[:, sl] = kh
        if attn_refs:
            attn_refs[0][:, sl] = kh.astype(BF16)

    zv = _dot(a, w_ref[:, 2 * aw:3 * aw])
    v_ref[...] = zv
    if attn_refs:
        attn_refs[1][...] = zv.T.astype(BF16)

    zu = _dot(a, w_ref[:, 3 * aw:3 * aw + MLP_WIDTH])
    u_ref[...] = jax.nn.gelu(zu)

    zm = _dot(a, w_ref[:, 3 * aw + MLP_WIDTH:3 * aw + 2 * MLP_WIDTH])
    for g, vg in enumerate(_group_rms(jax.nn.gelu(zm), vmn_ref[...])):
        vm_ref[:, g * HEAD_DIM:(g + 1) * HEAD_DIM] = vg

    zf = _dot(a, wf_ref[...]) + bf_ref[...]
    lf_ref[...] = jnp.minimum(zf, 0.0) - jnp.log1p(jnp.exp(-jnp.abs(zf)))


def _proj(h, nw, w_main, w_f, b_f, qn, kn, vmn, tm, q_scale, for_attn):
    m, d = h.shape
    row = lambda w: pl.BlockSpec((tm, w), lambda i: (i, 0))
    small = lambda w: pl.BlockSpec((1, w), lambda i: (0, 0))
    out_w = [(ATTN_WIDTH, BF16), (ATTN_WIDTH, F32), (ATTN_WIDTH, F32), (HEAD_DIM, F32),
             (MLP_WIDTH, F32), (MLP_WIDTH, F32)]
    out_specs = [row(w) for w, _ in out_w]
    out_shape = [jax.ShapeDtypeStruct((m, w), dt) for w, dt in out_w]
    if for_attn:
        out_specs += [row(ATTN_WIDTH), pl.BlockSpec((ATTN_WIDTH, tm), lambda i: (0, i))]
        out_shape += [jax.ShapeDtypeStruct((m, ATTN_WIDTH), BF16),
                      jax.ShapeDtypeStruct((ATTN_WIDTH, m), BF16)]
    return pl.pallas_call(
        functools.partial(_proj_kernel, q_scale=q_scale),
        grid=(m // tm,),
        in_specs=[row(d), small(d), _resident(w_main.shape), _resident(w_f.shape),
                  small(HEAD_DIM), small(HEAD_DIM), small(HEAD_DIM), small(MLP_WIDTH)],
        out_specs=out_specs,
        out_shape=out_shape,
        compiler_params=_params("parallel"),
        name="proj",
    )(h, nw, w_main, w_f, b_f, qn, kn, vmn)


def _cumsum_kernel(lf_ref, f_ref):
    x = lf_ref[...]
    n = x.shape[0]
    row = lax.broadcasted_iota(jnp.int32, x.shape, 0)
    s = 1
    while s < n:
        x = x + jnp.where(row >= s, pltpu.roll(x, s, 0), 0.0)
        s *= 2
    f_ref[...] = x * LOG2E


def _cumsum(lf, seq):
    return pl.pallas_call(
        _cumsum_kernel,
        grid=(lf.shape[0] // seq,),
        in_specs=[pl.BlockSpec((seq, HEAD_DIM), lambda b: (b, 0))],
        out_specs=pl.BlockSpec((seq, HEAD_DIM), lambda b: (b, 0)),
        out_shape=jax.ShapeDtypeStruct(lf.shape, F32),
        compiler_params=_params("parallel"),
        name="cumsum",
    )(lf)


def _attn_kernel(q_ref, k_ref, vt_ref, f_ref, o_ref, fcol_ref, *, t, nq):
    h = pl.program_id(1)
    i = pl.program_id(2)
    q = q_ref[...]
    key = lax.broadcasted_iota(jnp.int32, (t, t), 0)
    qry = lax.broadcasted_iota(jnp.int32, (t, t), 1)
    nt = (((1,), (1,)), ((), ()))

    @pl.when(i == 0)
    def _():
        f = f_ref[...]
        lane = lax.broadcasted_iota(jnp.int32, f.shape, 1)
        fcol_ref[...] = jnp.sum(jnp.where(lane == h, f, 0.0), axis=-1, keepdims=True)

    def scores(lo, hi):
        return (lax.dot_general(k_ref[lo:hi, :], q, nt, preferred_element_type=F32)
                - fcol_ref[lo:hi, :])

    def query_block(n):
        past = n - t
        s_diag = jnp.where(key <= qry, scores(past, n), -jnp.inf)
        m = jnp.max(s_diag, axis=0, keepdims=True)
        if past:
            s_past = scores(0, past)
            m = jnp.maximum(m, jnp.max(s_past, axis=0, keepdims=True))
        p = jnp.exp2(s_diag - m)
        l = jnp.sum(p, axis=0, keepdims=True)
        acc = _dot(vt_ref[:, past:n], p.astype(BF16))
        if past:
            p = jnp.exp2(s_past - m)
            l = l + jnp.sum(p, axis=0, keepdims=True)
            acc = acc + _dot(vt_ref[:, 0:past], p.astype(BF16))
        o_ref[...] = (acc / l).T.astype(o_ref.dtype)

    for blk in range(nq):
        pl.when(i == blk)(functools.partial(query_block, (blk + 1) * t))


def _attn(q, kb, vt, f, batch, seq):
    t = ATT_TILE
    nq = seq // t
    return pl.pallas_call(
        functools.partial(_attn_kernel, t=t, nq=nq),
        grid=(batch, N_HEADS, nq),
        in_specs=[
            pl.BlockSpec((t, HEAD_DIM), lambda b, h, i: (b * nq + i, h)),
            pl.BlockSpec((seq, HEAD_DIM), lambda b, h, i: (b, h)),
            pl.BlockSpec((HEAD_DIM, seq), lambda b, h, i: (h, b)),
            pl.BlockSpec((seq, HEAD_DIM), lambda b, h, i: (b, 0)),
        ],
        out_specs=pl.BlockSpec((t, HEAD_DIM), lambda b, h, i: (b * nq + i, h)),
        out_shape=jax.ShapeDtypeStruct((batch * seq, ATTN_WIDTH), BF16),
        scratch_shapes=[pltpu.VMEM((seq, 1), F32)],
        compiler_params=_params("parallel", "parallel", "arbitrary"),
        name="attn",
    )(q, kb, vt, f)


def _mixout_kernel(att_ref, u_ref, vm_ref, sw_ref, sbt_ref, wo_ref, h_ref, o_ref, mlp_ref):
    tm = att_ref.shape[0]
    r = lax.broadcasted_iota(jnp.int32, (CHUNK, CHUNK), 0)
    c = lax.broadcasted_iota(jnp.int32, (CHUNK, CHUNK), 1)
    for g in range(N_GROUPS):
        w = jnp.where(r >= c, sw_ref[g], 0.0).astype(BF16)
        bias = sbt_ref[:, g:g + 1]
        cols = slice(g * HEAD_DIM, (g + 1) * HEAD_DIM)
        for n in range(tm // CHUNK):
            rows = slice(n * CHUNK, (n + 1) * CHUNK)
            mixed = _dot(w, vm_ref[rows, cols].astype(BF16)) + bias
            mlp_ref[rows, cols] = (u_ref[rows, cols] * mixed).astype(BF16)
    o_ref[...] = (h_ref[...] + _dot(att_ref[...], wo_ref[0:ATTN_WIDTH, :])
                  + _dot(mlp_ref[...], wo_ref[ATTN_WIDTH:, :]))


def _mixout(att, u, vm, sw, sbt, wo, h, tm):
    m, d = h.shape
    row = lambda w: pl.BlockSpec((tm, w), lambda i: (i, 0))
    return pl.pallas_call(
        _mixout_kernel,
        grid=(m // tm,),
        in_specs=[row(ATTN_WIDTH), row(MLP_WIDTH), row(MLP_WIDTH), _resident(sw.shape),
                  _resident(sbt.shape), _resident(wo.shape), row(d)],
        out_specs=row(d),
        out_shape=jax.ShapeDtypeStruct((m, d), F32),
        scratch_shapes=[pltpu.VMEM((tm, MLP_WIDTH), BF16)],
        compiler_params=_params("parallel"),
        name="mixout",
    )(att, u, vm, sw, sbt, wo, h)


def _mixout_one_kernel(att_ref, u_ref, vm_ref, w00_ref, b0_ref, wo_ref, h_ref, o_ref):
    mlp = u_ref[...] * (w00_ref[...] * vm_ref[...] + b0_ref[...])
    o_ref[...] = (h_ref[...] + _dot(att_ref[...].astype(BF16), wo_ref[0:ATTN_WIDTH, :])
                  + _dot(mlp.astype(BF16), wo_ref[ATTN_WIDTH:, :]))


def _mixout_one(att, u, vm, w00, b0, wo, h):
    m, d = h.shape
    full = lambda a: pl.BlockSpec(a.shape, lambda i: (0,) * a.ndim)
    args = (att, u, vm, w00, b0, wo, h)
    return pl.pallas_call(
        _mixout_one_kernel,
        grid=(1,),
        in_specs=[full(a) for a in args],
        out_specs=pl.BlockSpec((m, d), lambda i: (0, 0)),
        out_shape=jax.ShapeDtypeStruct((m, d), F32),
        compiler_params=_params("arbitrary"),
        name="mixout_one",
    )(*args)


def _decode_pages(q, state, ck, cv, clf):
    m_old, l_old, acc_old, carry = state
    sub = lax.broadcasted_iota(jnp.int32, (N_HEADS, HEAD_DIM), 0)
    lane = lax.broadcasted_iota(jnp.int32, (N_HEADS, PAGE), 1)

    def head_rows(page_ref, h):
        return page_ref[pl.ds(h, PAGE, stride=N_HEADS), :].astype(BF16)

    def own_head(per_head):
        out = per_head(0)
        for h in range(1, N_HEADS):
            out = jnp.where(sub == h, per_head(h), out)
        return out

    logits = []
    for j in range(len(ck)):
        lf = clf[j][...]
        y = lf
        s = 1
        while s < PAGE:
            y = y + jnp.where(lane + s < PAGE, pltpu.roll(y, PAGE - s, 1), 0.0)
            s *= 2
        decay = y - lf + carry
        carry = carry + y[:, 0:1]
        scores = own_head(lambda h: lax.dot_general(
            q, head_rows(ck[j], h), (((1,), (1,)), ((), ())),
            preferred_element_type=F32))
        logits.append(scores + decay)

    m_new = m_old
    for lg in logits:
        m_new = jnp.maximum(m_new, jnp.max(lg, axis=-1, keepdims=True))
    alpha = jnp.exp(m_old - m_new)
    l = alpha * l_old
    acc = alpha * acc_old
    for j in range(len(cv)):
        pr = jnp.exp(logits[j] - m_new)
        l = l + jnp.sum(pr, axis=-1, keepdims=True)
        prb = pr.astype(BF16)
        acc = acc + own_head(lambda h: _dot(prb, head_rows(cv[j], h)))
    return m_new, l, acc, carry


def _decode_kernel(pt_ref, q_ref, kn_ref, vn_ref, lfn_ref, *refs, n_steps):
    g = DEC_PAGES
    ck, cv, clf = refs[0:g], refs[g:2 * g], refs[2 * g:3 * g]
    o_ref, m_ref, l_ref, acc_ref, c_ref = refs[3 * g:]
    step = pl.program_id(1)
    q = q_ref[...]

    @pl.when(step == 0)
    def _():
        kn = kn_ref[...].astype(BF16).astype(F32)
        m_ref[...] = jnp.sum(q.astype(F32) * kn, axis=-1, keepdims=True)
        l_ref[...] = jnp.ones_like(l_ref)
        acc_ref[...] = vn_ref[...]
        c_ref[...] = lfn_ref[...]

    state = (m_ref[...], l_ref[...], acc_ref[...], c_ref[...])
    m_ref[...], l_ref[...], acc_ref[...], c_ref[...] = _decode_pages(q, state, ck, cv, clf)

    @pl.when(step == n_steps - 1)
    def _():
        o_ref[...] = acc_ref[...] / l_ref[...]


def _decode(page_table, q, kn, vn, lfn, ck, cv, clft):
    nb, n_pages = page_table.shape
    g = DEC_PAGES
    n_steps = n_pages // g
    per_b = lambda b, s, pt: (b, 0, 0)

    def page(j):
        return lambda b, s, pt: (pt[b, n_pages - 1 - (s * g + j)], 0, 0)

    kv_spec = lambda j: pl.BlockSpec((None, PAGE * N_HEADS, HEAD_DIM), page(j))
    lf_spec = lambda j: pl.BlockSpec((None, N_HEADS, PAGE), page(j))
    head_block = pl.BlockSpec((None, N_HEADS, HEAD_DIM), per_b)
    grid_spec = pltpu.PrefetchScalarGridSpec(
        num_scalar_prefetch=1,
        grid=(nb, n_steps),
        in_specs=([head_block, head_block, head_block, pl.BlockSpec((None, N_HEADS, 1), per_b)]
                  + [kv_spec(j) for j in range(g)] + [kv_spec(j) for j in range(g)]
                  + [lf_spec(j) for j in range(g)]),
        out_specs=head_block,
        scratch_shapes=[pltpu.VMEM((N_HEADS, 1), F32), pltpu.VMEM((N_HEADS, 1), F32),
                        pltpu.VMEM((N_HEADS, HEAD_DIM), F32), pltpu.VMEM((N_HEADS, 1), F32)],
    )
    return pl.pallas_call(
        functools.partial(_decode_kernel, n_steps=n_steps),
        grid_spec=grid_spec,
        out_shape=jax.ShapeDtypeStruct((nb, N_HEADS, HEAD_DIM), F32),
        compiler_params=_params("parallel", "arbitrary"),
        name="decode",
    )(page_table, q, kn, vn, lfn, *([ck] * g), *([cv] * g), *([clft] * g))


def _ple_kernel(h_ref, p_ref, nw_ref, wg_ref, wp_ref, o_ref):
    h = h_ref[...]
    gate = jax.nn.sigmoid(_dot(_rms(h, nw_ref[...]).astype(BF16), wg_ref[...]))
    o_ref[...] = h + gate * _dot(p_ref[...].astype(BF16), wp_ref[...])


def _ple(h, p, nw, wg, wp, tm):
    m, d = h.shape
    return pl.pallas_call(
        _ple_kernel,
        grid=(m // tm,),
        in_specs=[pl.BlockSpec((tm, d), lambda i: (i, 0)),
                  pl.BlockSpec((tm, p.shape[1]), lambda i: (i, 0)),
                  pl.BlockSpec((1, d), lambda i: (0, 0)),
                  _resident(wg.shape), _resident(wp.shape)],
        out_specs=pl.BlockSpec((tm, d), lambda i: (i, 0)),
        out_shape=jax.ShapeDtypeStruct((m, d), F32),
        compiler_params=_params("parallel"),
        name="ple",
    )(h, p, nw, wg, wp)


def _pad_cols(w, mult):
    return jnp.pad(w, ((0, 0), (0, (-w.shape[1]) % mult)))


def kernel(x_prompt, x_sample, p_prompt, p_sample, cache_k, cache_v, cache_logf, page_table,
           ffn1_norm, ffn1_w_gate, ffn1_w_up, ffn1_w_down, mix_norm, w_in, b_f, q_norm, k_norm,
           v_mlp_norm, spatial_w, spatial_b, w_o, ffn2_norm, ffn2_w_gate, ffn2_w_up, ffn2_w_down,
           ple_norm, w_ple_gate, w_ple_proj):
    batch, seq, d = x_prompt.shape
    nb, dec_seq, _ = x_sample.shape
    depth = ffn1_norm.shape[0]
    m = batch * seq
    assert depth == 1 and dec_seq == 1
    assert seq % ATT_TILE == 0 and m % ROW_TILE == 0 and m % PROJ_TILE == 0
    assert page_table.shape[1] % DEC_PAGES == 0
    ffn_tm = FFN_ROW_TILE if m % FFN_ROW_TILE == 0 else ROW_TILE
    l = 0

    def ffn_weights(wg, wu, wd):
        return _cast_padded(wg, 1), _cast_padded(wu, 1), _cast_padded(wd, 0)

    ffn1 = (ffn1_norm[l][None],) + ffn_weights(ffn1_w_gate[l], ffn1_w_up[l], ffn1_w_down[l])
    ffn2 = (ffn2_norm[l][None],) + ffn_weights(ffn2_w_gate[l], ffn2_w_up[l], ffn2_w_down[l])
    n_main = 3 * ATTN_WIDTH + 2 * MLP_WIDTH
    proj = (mix_norm[l][None], w_in[l][:, :n_main].astype(BF16),
            _pad_cols(w_in[l][:, n_main:], HEAD_DIM).astype(BF16),
            _pad_cols(b_f[l][None], HEAD_DIM), q_norm[l][None], k_norm[l][None],
            v_mlp_norm[l].reshape(1, MLP_WIDTH))
    wo = w_o[l].astype(BF16)
    ple = (ple_norm[l][None], w_ple_gate[l].astype(BF16), w_ple_proj[l].astype(BF16))
    heads = (N_HEADS, HEAD_DIM)

    hp = _ffn(x_prompt.reshape(m, d), *ffn1, tm=ffn_tm)
    q, k, v, lf, u, vm, kb, vt = _proj(hp, *proj, tm=PROJ_TILE, q_scale=QK_SCALE * LOG2E,
                                       for_attn=True)
    att = _attn(q, kb, vt, _cumsum(lf, seq), batch, seq)
    hp = _mixout(att, u, vm, spatial_w[l], spatial_b[l].T, wo, hp, tm=ROW_TILE)
    hp = _ffn(hp, *ffn2, tm=ffn_tm)
    hp = _ple(hp, p_prompt[l].reshape(m, -1), *ple, tm=ROW_TILE)

    hs = _ffn(x_sample.reshape(nb, d), *ffn1, tm=nb)
    qs, ks, vs, lfs, us, vms = _proj(hs, *proj, tm=nb, q_scale=QK_SCALE, for_attn=False)
    att_s = _decode(page_table, qs.reshape(nb, *heads), ks.reshape(nb, *heads),
                    vs.reshape(nb, *heads), lfs[:, :N_HEADS].reshape(nb, N_HEADS, 1),
                    cache_k[l].reshape(-1, PAGE * N_HEADS, HEAD_DIM),
                    cache_v[l].reshape(-1, PAGE * N_HEADS, HEAD_DIM),
                    jnp.swapaxes(cache_logf[l], 1, 2))
    w00 = jnp.repeat(spatial_w[l][:, 0, 0], HEAD_DIM)[None]
    b0 = jnp.repeat(spatial_b[l][:, 0], HEAD_DIM)[None]
    hs = _mixout_one(att_s.reshape(nb, ATTN_WIDTH), us, vms, w00, b0, wo, hs)
    hs = _ffn(hs, *ffn2, tm=nb)
    hs = _ple(hs, p_sample[l].reshape(nb, -1), *ple, tm=nb)

    return (hp.reshape(batch, seq, d), hs.reshape(nb, 1, d),
            k.reshape(1, batch, seq, *heads), v.reshape(1, batch, seq, *heads),
            lf[:, :N_HEADS].reshape(1, batch, seq, N_HEADS),
            ks.reshape(1, nb, 1, *heads), vs.reshape(1, nb, 1, *heads),
            lfs[:, :N_HEADS].reshape(1, nb, 1, N_HEADS),
            vms.reshape(1, nb, 1, N_GROUPS, HEAD_DIM))
```

```python
import functools

import jax
import jax.numpy as jnp
from jax import lax
from jax.experimental import pallas as pl
from jax.experimental.pallas import tpu as pltpu

F32 = jnp.float32
BF16 = jnp.bfloat16

HEAD_DIM = 128
N_HEADS = 8
N_GROUPS = 8
ATTN_WIDTH = N_HEADS * HEAD_DIM
MLP_WIDTH = N_GROUPS * HEAD_DIM
CHUNK = 128
PAGE = 128
RMS_EPS = 1e-6
QK_SCALE = HEAD_DIM ** -0.5
LOG2E = 1.4426950408889634

V7X_VMEM_BYTES = 64 * 1024 * 1024
VMEM_LIMIT = V7X_VMEM_BYTES - 8 * 1024 * 1024

FF_TILE = 512
FFN_ROW_TILE = 1024
ROW_TILE = 512
PROJ_TILE = 512
ATT_TILE = 512
DEC_PAGES = 16


def _params(*sem):
    return pltpu.CompilerParams(dimension_semantics=sem, vmem_limit_bytes=VMEM_LIMIT)


def _resident(shape):
    zeros = (0,) * len(shape)
    return pl.BlockSpec(shape, lambda *_: zeros, pipeline_mode=pl.Buffered(1))


def _rms(x, w):
    return x * lax.rsqrt(jnp.mean(x * x, axis=-1, keepdims=True) + RMS_EPS) * w


def _dot(a, b):
    return jnp.dot(a, b, preferred_element_type=F32)


def _cast_kernel(w_ref, o_ref, *, axis, size, tile):
    pos = pl.program_id(0) * tile + lax.broadcasted_iota(jnp.int32, w_ref.shape, axis)
    o_ref[...] = jnp.where(pos < size, w_ref[...], 0.0).astype(BF16)


def _cast_padded(w, axis):
    size = w.shape[axis]
    n = pl.cdiv(size, FF_TILE)
    block = tuple(FF_TILE if a == axis else s for a, s in enumerate(w.shape))
    index = (lambda i: (i, 0)) if axis == 0 else (lambda i: (0, i))
    out_shape = tuple(n * FF_TILE if a == axis else s for a, s in enumerate(w.shape))
    return pl.pallas_call(
        functools.partial(_cast_kernel, axis=axis, size=size, tile=FF_TILE),
        grid=(n,),
        in_specs=[pl.BlockSpec(block, index)],
        out_specs=pl.BlockSpec(block, index),
        out_shape=jax.ShapeDtypeStruct(out_shape, BF16),
        compiler_params=_params("parallel"),
        name="cast",
    )(w)


def _ffn_kernel(x_ref, nw_ref, wg_ref, wu_ref, wd_ref, o_ref, xn_ref, *, nf):
    f = pl.program_id(1)

    @pl.when(f == 0)
    def _():
        xn_ref[...] = _rms(x_ref[...], nw_ref[...]).astype(BF16)
        o_ref[...] = jnp.zeros_like(o_ref)

    xn = xn_ref[...]
    g = _dot(xn, wg_ref[...])
    u = _dot(xn, wu_ref[...])
    h = (jax.nn.silu(g) * u).astype(BF16)
    o_ref[...] += _dot(h, wd_ref[...])

    @pl.when(f == nf - 1)
    def _():
        o_ref[...] = x_ref[...] + 0.5 * o_ref[...]


def _ffn(x, nw, wg, wu, wd, tm):
    m, d = x.shape
    nf = wg.shape[1] // FF_TILE
    return pl.pallas_call(
        functools.partial(_ffn_kernel, nf=nf),
        grid=(m // tm, nf),
        in_specs=[
            pl.BlockSpec((tm, d), lambda i, f: (i, 0)),
            pl.BlockSpec((1, d), lambda i, f: (0, 0)),
            pl.BlockSpec((d, FF_TILE), lambda i, f: (0, f)),
            pl.BlockSpec((d, FF_TILE), lambda i, f: (0, f)),
            pl.BlockSpec((FF_TILE, d), lambda i, f: (f, 0)),
        ],
        out_specs=pl.BlockSpec((tm, d), lambda i, f: (i, 0)),
        out_shape=jax.ShapeDtypeStruct((m, d), F32),
        scratch_shapes=[pltpu.VMEM((tm, d), BF16)],
        compiler_params=_params("parallel", "arbitrary"),
        name="ffn",
    )(x, nw, wg, wu, wd)


def _group_rms(z, gain):
    outs = []
    for h in range(z.shape[1] // HEAD_DIM):
        sl = slice(h * HEAD_DIM, (h + 1) * HEAD_DIM)
        g = gain if gain.shape[1] == HEAD_DIM else gain[:, sl]
        outs.append(_rms(z[:, sl], g))
    return outs


def _proj_kernel(h_ref, nw_ref, w_ref, bf_ref, qn_ref, kn_ref, vmn_ref,
                 q_ref, k_ref, v_ref, lf_ref, u_ref, vm_ref, *attn_refs, q_scale):
    a = _rms(h_ref[...], nw_ref[...]).astype(BF16)
    aw = ATTN_WIDTH

    zq = _dot(a, w_ref[:, 0:aw])
    for h, qh in enumerate(_group_rms(zq, qn_ref[...])):
        q_ref[:, h * HEAD_DIM:(h + 1) * HEAD_DIM] = (qh * q_scale).astype(BF16)

    zk = _dot(a, w_ref[:, aw:2 * aw])
    for h, kh in enumerate(_group_rms(zk, kn_ref[...])):
        sl = slice(h * HEAD_DIM, (h + 1) * HEAD_DIM)
        k_ref[:, sl] = kh
        if attn_refs:
            attn_refs[0][:, sl] = kh.astype(BF16)

    zv = _dot(a, w_ref[:, 2 * aw:3 * aw])
    v_ref[...] = zv
    if attn_refs:
        attn_refs[1][...] = zv.T.astype(BF16)

    zu = _dot(a, w_ref[:, 3 * aw:3 * aw + MLP_WIDTH])
    u_ref[...] = jax.nn.gelu(zu).astype(u_ref.dtype)

    zm = _dot(a, w_ref[:, 3 * aw + MLP_WIDTH:3 * aw + 2 * MLP_WIDTH])
    for g, vg in enumerate(_group_rms(jax.nn.gelu(zm), vmn_ref[...])):
        vm_ref[:, g * HEAD_DIM:(g + 1) * HEAD_DIM] = vg.astype(vm_ref.dtype)

    n_main = 3 * aw + 2 * MLP_WIDTH
    zf = _dot(a, w_ref[:, n_main:n_main + HEAD_DIM]) + bf_ref[...]
    lf_ref[...] = jnp.minimum(zf, 0.0) - jnp.log1p(jnp.exp(-jnp.abs(zf)))


def _proj(h, nw, w_in, b_f, qn, kn, vmn, tm, q_scale, for_attn):
    m, d = h.shape
    row = lambda w: pl.BlockSpec((tm, w), lambda i: (i, 0))
    small = lambda w: pl.BlockSpec((1, w), lambda i: (0, 0))
    mlp_dt = BF16 if for_attn else F32
    out_w = [(ATTN_WIDTH, BF16), (ATTN_WIDTH, F32), (ATTN_WIDTH, F32), (HEAD_DIM, F32),
             (MLP_WIDTH, mlp_dt), (MLP_WIDTH, mlp_dt)]
    out_specs = [row(w) for w, _ in out_w]
    out_shape = [jax.ShapeDtypeStruct((m, w), dt) for w, dt in out_w]
    if for_attn:
        out_specs += [row(ATTN_WIDTH), pl.BlockSpec((ATTN_WIDTH, tm), lambda i: (0, i))]
        out_shape += [jax.ShapeDtypeStruct((m, ATTN_WIDTH), BF16),
                      jax.ShapeDtypeStruct((ATTN_WIDTH, m), BF16)]
    return pl.pallas_call(
        functools.partial(_proj_kernel, q_scale=q_scale),
        grid=(m // tm,),
        in_specs=[row(d), small(d), _resident(w_in.shape),
                  small(HEAD_DIM), small(HEAD_DIM), small(HEAD_DIM), small(MLP_WIDTH)],
        out_specs=out_specs,
        out_shape=out_shape,
        compiler_params=_params("parallel"),
        name="proj",
    )(h, nw, w_in, b_f, qn, kn, vmn)


def _cumsum_kernel(lf_ref, f_ref):
    x = lf_ref[...]
    n = x.shape[0]
    row = lax.broadcasted_iota(jnp.int32, x.shape, 0)
    s = 1
    while s < n:
        x = x + jnp.where(row >= s, pltpu.roll(x, s, 0), 0.0)
        s *= 2
    f_ref[...] = x * LOG2E


def _cumsum(lf, seq):
    return pl.pallas_call(
        _cumsum_kernel,
        grid=(lf.shape[0] // seq,),
        in_specs=[pl.BlockSpec((seq, HEAD_DIM), lambda b: (b, 0))],
        out_specs=pl.BlockSpec((seq, HEAD_DIM), lambda b: (b, 0)),
        out_shape=jax.ShapeDtypeStruct(lf.shape, F32),
        compiler_params=_params("parallel"),
        name="cumsum",
    )(lf)


def _attn_kernel(q_ref, k_ref, vt_ref, f_ref, o_ref, fcol_ref, *, t, nq):
    h = pl.program_id(1)
    key = lax.broadcasted_iota(jnp.int32, (t, t), 0)
    qry = lax.broadcasted_iota(jnp.int32, (t, t), 1)
    nt = (((1,), (1,)), ((), ()))

    f = f_ref[...]
    lane = lax.broadcasted_iota(jnp.int32, f.shape, 1)
    fcol_ref[...] = jnp.sum(jnp.where(lane == h, f, 0.0), axis=-1, keepdims=True)

    def scores(q, lo, hi):
        return (lax.dot_general(k_ref[lo:hi, :], q, nt, preferred_element_type=F32)
                - fcol_ref[lo:hi, :])

    for blk in range(nq):
        past, n = blk * t, (blk + 1) * t
        q = q_ref[past:n, :]
        s_diag = jnp.where(key <= qry, scores(q, past, n), -jnp.inf)
        m = jnp.max(s_diag, axis=0, keepdims=True)
        if past:
            s_past = scores(q, 0, past)
            m = jnp.maximum(m, jnp.max(s_past, axis=0, keepdims=True))
        p = jnp.exp2(s_diag - m)
        l = jnp.sum(p, axis=0, keepdims=True)
        acc = _dot(vt_ref[:, past:n], p.astype(BF16))
        if past:
            p = jnp.exp2(s_past - m)
            l = l + jnp.sum(p, axis=0, keepdims=True)
            acc = acc + _dot(vt_ref[:, 0:past], p.astype(BF16))
        o_ref[past:n, :] = (acc / l).T.astype(o_ref.dtype)


def _attn(q, kb, vt, f, batch, seq):
    t = ATT_TILE
    nq = seq // t
    return pl.pallas_call(
        functools.partial(_attn_kernel, t=t, nq=nq),
        grid=(batch, N_HEADS),
        in_specs=[
            pl.BlockSpec((seq, HEAD_DIM), lambda b, h: (b, h)),
            pl.BlockSpec((seq, HEAD_DIM), lambda b, h: (b, h)),
            pl.BlockSpec((HEAD_DIM, seq), lambda b, h: (h, b)),
            pl.BlockSpec((seq, HEAD_DIM), lambda b, h: (b, 0)),
        ],
        out_specs=pl.BlockSpec((seq, HEAD_DIM), lambda b, h: (b, h)),
        out_shape=jax.ShapeDtypeStruct((batch * seq, ATTN_WIDTH), BF16),
        scratch_shapes=[pltpu.VMEM((seq, 1), F32)],
        compiler_params=_params("parallel", "arbitrary"),
        name="attn",
    )(q, kb, vt, f)


def _mixout_kernel(att_ref, u_ref, vm_ref, sw_ref, sbt_ref, wo_ref, h_ref, o_ref, mlp_ref):
    tm = att_ref.shape[0]
    r = lax.broadcasted_iota(jnp.int32, (CHUNK, CHUNK), 0)
    c = lax.broadcasted_iota(jnp.int32, (CHUNK, CHUNK), 1)
    for g in range(N_GROUPS):
        w = jnp.where(r >= c, sw_ref[g], 0.0).astype(BF16)
        bias = sbt_ref[:, g:g + 1]
        cols = slice(g * HEAD_DIM, (g + 1) * HEAD_DIM)
        for n in range(tm // CHUNK):
            rows = slice(n * CHUNK, (n + 1) * CHUNK)
            mixed = _dot(w, vm_ref[rows, cols].astype(BF16)) + bias
            mlp_ref[rows, cols] = (u_ref[rows, cols] * mixed).astype(BF16)
    o_ref[...] = (h_ref[...] + _dot(att_ref[...], wo_ref[0:ATTN_WIDTH, :])
                  + _dot(mlp_ref[...], wo_ref[ATTN_WIDTH:, :]))


def _mixout(att, u, vm, sw, sbt, wo, h, tm):
    m, d = h.shape
    row = lambda w: pl.BlockSpec((tm, w), lambda i: (i, 0))
    return pl.pallas_call(
        _mixout_kernel,
        grid=(m // tm,),
        in_specs=[row(ATTN_WIDTH), row(MLP_WIDTH), row(MLP_WIDTH), _resident(sw.shape),
                  _resident(sbt.shape), _resident(wo.shape), row(d)],
        out_specs=row(d),
        out_shape=jax.ShapeDtypeStruct((m, d), F32),
        scratch_shapes=[pltpu.VMEM((tm, MLP_WIDTH), BF16)],
        compiler_params=_params("parallel"),
        name="mixout",
    )(att, u, vm, sw, sbt, wo, h)


def _mixout_one_kernel(att_ref, u_ref, vm_ref, w00_ref, b0_ref, wo_ref, h_ref, o_ref):
    mlp = u_ref[...] * (w00_ref[...] * vm_ref[...] + b0_ref[...])
    o_ref[...] = (h_ref[...] + _dot(att_ref[...].astype(BF16), wo_ref[0:ATTN_WIDTH, :])
                  + _dot(mlp.astype(BF16), wo_ref[ATTN_WIDTH:, :]))


def _mixout_one(att, u, vm, w00, b0, wo, h):
    m, d = h.shape
    full = lambda a: pl.BlockSpec(a.shape, lambda i: (0,) * a.ndim)
    args = (att, u, vm, w00, b0, wo, h)
    return pl.pallas_call(
        _mixout_one_kernel,
        grid=(1,),
        in_specs=[full(a) for a in args],
        out_specs=pl.BlockSpec((m, d), lambda i: (0, 0)),
        out_shape=jax.ShapeDtypeStruct((m, d), F32),
        compiler_params=_params("arbitrary"),
        name="mixout_one",
    )(*args)


def _decode_pages(q, state, ck, cv, clf):
    m_old, l_old, acc_old, carry = state
    sub = lax.broadcasted_iota(jnp.int32, (N_HEADS, HEAD_DIM), 0)
    lane = lax.broadcasted_iota(jnp.int32, (N_HEADS, PAGE), 1)

    def head_rows(page_ref, h):
        return page_ref[pl.ds(h, PAGE, stride=N_HEADS), :].astype(BF16)

    def own_head(per_head):
        out = per_head(0)
        for h in range(1, N_HEADS):
            out = jnp.where(sub == h, per_head(h), out)
        return out

    logits = []
    for j in range(len(ck)):
        lf = clf[j][...]
        y = lf
        s = 1
        while s < PAGE:
            y = y + jnp.where(lane + s < PAGE, pltpu.roll(y, PAGE - s, 1), 0.0)
            s *= 2
        decay = y - lf + carry
        carry = carry + y[:, 0:1]
        scores = own_head(lambda h: lax.dot_general(
            q, head_rows(ck[j], h), (((1,), (1,)), ((), ())),
            preferred_element_type=F32))
        logits.append(scores + decay)

    m_new = m_old
    for lg in logits:
        m_new = jnp.maximum(m_new, jnp.max(lg, axis=-1, keepdims=True))
    alpha = jnp.exp(m_old - m_new)
    l = alpha * l_old
    acc = alpha * acc_old
    for j in range(len(cv)):
        pr = jnp.exp(logits[j] - m_new)
        l = l + jnp.sum(pr, axis=-1, keepdims=True)
        prb = pr.astype(BF16)
        acc = acc + own_head(lambda h: _dot(prb, head_rows(cv[j], h)))
    return m_new, l, acc, carry


def _decode_kernel(pt_ref, q_ref, kn_ref, vn_ref, lfn_ref, *refs, n_steps):
    g = DEC_PAGES
    ck, cv, clf = refs[0:g], refs[g:2 * g], refs[2 * g:3 * g]
    o_ref, m_ref, l_ref, acc_ref, c_ref = refs[3 * g:]
    step = pl.program_id(1)
    q = q_ref[...]

    @pl.when(step == 0)
    def _():
        kn = kn_ref[...].astype(BF16).astype(F32)
        m_ref[...] = jnp.sum(q.astype(F32) * kn, axis=-1, keepdims=True)
        l_ref[...] = jnp.ones_like(l_ref)
        acc_ref[...] = vn_ref[...]
        c_ref[...] = lfn_ref[...]

    state = (m_ref[...], l_ref[...], acc_ref[...], c_ref[...])
    m_ref[...], l_ref[...], acc_ref[...], c_ref[...] = _decode_pages(q, state, ck, cv, clf)

    @pl.when(step == n_steps - 1)
    def _():
        o_ref[...] = acc_ref[...] / l_ref[...]


def _decode(page_table, q, kn, vn, lfn, ck, cv, clft):
    nb, n_pages = page_table.shape
    g = DEC_PAGES
    n_steps = n_pages // g
    per_b = lambda b, s, pt: (b, 0, 0)

    def page(j):
        return lambda b, s, pt: (pt[b, n_pages - 1 - (s * g + j)], 0, 0)

    kv_spec = lambda j: pl.BlockSpec((None, PAGE * N_HEADS, HEAD_DIM), page(j))
    lf_spec = lambda j: pl.BlockSpec((None, N_HEADS, PAGE), page(j))
    head_block = pl.BlockSpec((None, N_HEADS, HEAD_DIM), per_b)
    grid_spec = pltpu.PrefetchScalarGridSpec(
        num_scalar_prefetch=1,
        grid=(nb, n_steps),
        in_specs=([head_block, head_block, head_block, pl.BlockSpec((None, N_HEADS, 1), per_b)]
                  + [kv_spec(j) for j in range(g)] + [kv_spec(j) for j in range(g)]
                  + [lf_spec(j) for j in range(g)]),
        out_specs=head_block,
        scratch_shapes=[pltpu.VMEM((N_HEADS, 1), F32), pltpu.VMEM((N_HEADS, 1), F32),
                        pltpu.VMEM((N_HEADS, HEAD_DIM), F32), pltpu.VMEM((N_HEADS, 1), F32)],
    )
    return pl.pallas_call(
        functools.partial(_decode_kernel, n_steps=n_steps),
        grid_spec=grid_spec,
        out_shape=jax.ShapeDtypeStruct((nb, N_HEADS, HEAD_DIM), F32),
        compiler_params=_params("parallel", "arbitrary"),
        name="decode",
    )(page_table, q, kn, vn, lfn, *([ck] * g), *([cv] * g), *([clft] * g))


def _ple_kernel(h_ref, p_ref, nw_ref, wg_ref, wp_ref, o_ref):
    h = h_ref[...]
    gate = jax.nn.sigmoid(_dot(_rms(h, nw_ref[...]).astype(BF16), wg_ref[...]))
    o_ref[...] = h + gate * _dot(p_ref[...].astype(BF16), wp_ref[...])


def _ple(h, p, nw, wg, wp, tm):
    m, d = h.shape
    return pl.pallas_call(
        _ple_kernel,
        grid=(m // tm,),
        in_specs=[pl.BlockSpec((tm, d), lambda i: (i, 0)),
                  pl.BlockSpec((tm, p.shape[1]), lambda i: (i, 0)),
                  pl.BlockSpec((1, d), lambda i: (0, 0)),
                  _resident(wg.shape), _resident(wp.shape)],
        out_specs=pl.BlockSpec((tm, d), lambda i: (i, 0)),
        out_shape=jax.ShapeDtypeStruct((m, d), F32),
        compiler_params=_params("parallel"),
        name="ple",
    )(h, p, nw, wg, wp)


def _pad_cols(w, mult):
    return jnp.pad(w, ((0, 0), (0, (-w.shape[1]) % mult)))


def kernel(x_prompt, x_sample, p_prompt, p_sample, cache_k, cache_v, cache_logf, page_table,
           ffn1_norm, ffn1_w_gate, ffn1_w_up, ffn1_w_down, mix_norm, w_in, b_f, q_norm, k_norm,
           v_mlp_norm, spatial_w, spatial_b, w_o, ffn2_norm, ffn2_w_gate, ffn2_w_up, ffn2_w_down,
           ple_norm, w_ple_gate, w_ple_proj):
    batch, seq, d = x_prompt.shape
    nb, dec_seq, _ = x_sample.shape
    depth = ffn1_norm.shape[0]
    m = batch * seq
    assert depth == 1 and dec_seq == 1
    assert seq % ATT_TILE == 0 and m % ROW_TILE == 0 and m % PROJ_TILE == 0
    assert page_table.shape[1] % DEC_PAGES == 0
    ffn_tm = FFN_ROW_TILE if m % FFN_ROW_TILE == 0 else ROW_TILE
    l = 0

    def ffn_weights(wg, wu, wd):
        return _cast_padded(wg, 1), _cast_padded(wu, 1), _cast_padded(wd, 0)

    ffn1 = (ffn1_norm[l][None],) + ffn_weights(ffn1_w_gate[l], ffn1_w_up[l], ffn1_w_down[l])
    ffn2 = (ffn2_norm[l][None],) + ffn_weights(ffn2_w_gate[l], ffn2_w_up[l], ffn2_w_down[l])
    proj = (mix_norm[l][None], _cast_padded(w_in[l], 1),
            _pad_cols(b_f[l][None], HEAD_DIM), q_norm[l][None], k_norm[l][None],
            v_mlp_norm[l].reshape(1, MLP_WIDTH))
    wo = w_o[l].astype(BF16)
    ple = (ple_norm[l][None], w_ple_gate[l].astype(BF16), w_ple_proj[l].astype(BF16))
    heads = (N_HEADS, HEAD_DIM)

    hp = _ffn(x_prompt.reshape(m, d), *ffn1, tm=ffn_tm)
    q, k, v, lf, u, vm, kb, vt = _proj(hp, *proj, tm=PROJ_TILE, q_scale=QK_SCALE * LOG2E,
                                       for_attn=True)
    att = _attn(q, kb, vt, _cumsum(lf, seq), batch, seq)
    hp = _mixout(att, u, vm, spatial_w[l], spatial_b[l].T, wo, hp, tm=ROW_TILE)
    hp = _ffn(hp, *ffn2, tm=ffn_tm)
    hp = _ple(hp, p_prompt[l].reshape(m, -1), *ple, tm=ROW_TILE)

    hs = _ffn(x_sample.reshape(nb, d), *ffn1, tm=nb)
    qs, ks, vs, lfs, us, vms = _proj(hs, *proj, tm=nb, q_scale=QK_SCALE, for_attn=False)
    att_s = _decode(page_table, qs.reshape(nb, *heads), ks.reshape(nb, *heads),
                    vs.reshape(nb, *heads), lfs[:, :N_HEADS].reshape(nb, N_HEADS, 1),
                    cache_k[l].reshape(-1, PAGE * N_HEADS, HEAD_DIM),
                    cache_v[l].reshape(-1, PAGE * N_HEADS, HEAD_DIM),
                    jnp.swapaxes(cache_logf[l], 1, 2))
    w00 = jnp.repeat(spatial_w[l][:, 0, 0], HEAD_DIM)[None]
    b0 = jnp.repeat(spatial_b[l][:, 0], HEAD_DIM)[None]
    hs = _mixout_one(att_s.reshape(nb, ATTN_WIDTH), us, vms, w00, b0, wo, hs)
    hs = _ffn(hs, *ffn2, tm=nb)
    hs = _ple(hs, p_sample[l].reshape(nb, -1), *ple, tm=nb)

    return (hp.reshape(batch, seq, d), hs.reshape(nb, 1, d),
            k.reshape(1, batch, seq, *heads), v.reshape(1, batch, seq, *heads),
            lf[:, :N_HEADS].reshape(1, batch, seq, N_HEADS),
            ks.reshape(1, nb, 1, *heads), vs.reshape(1, nb, 1, *heads),
            lfs[:, :N_HEADS].reshape(1, nb, 1, N_HEADS),
            vms.reshape(1, nb, 1, N_GROUPS, HEAD_DIM))
```

```python
import functools

import jax
import jax.numpy as jnp
from jax import lax
from jax.experimental import pallas as pl
from jax.experimental.pallas import tpu as pltpu

F32 = jnp.float32
BF16 = jnp.bfloat16

HEAD_DIM = 128
N_HEADS = 8
N_GROUPS = 8
ATTN_WIDTH = N_HEADS * HEAD_DIM
MLP_WIDTH = N_GROUPS * HEAD_DIM
CHUNK = 128
PAGE = 128
RMS_EPS = 1e-6
QK_SCALE = HEAD_DIM ** -0.5
LOG2E = 1.4426950408889634

V7X_VMEM_BYTES = 64 * 1024 * 1024
VMEM_LIMIT = V7X_VMEM_BYTES - 8 * 1024 * 1024

FF_TILE = 512
FFN_ROW_TILE = 1024
ROW_TILE = 512
PROJ_TILE = 512
ATT_TILE = 512
DEC_PAGES = 16


def _params(*sem):
    return pltpu.CompilerParams(dimension_semantics=sem, vmem_limit_bytes=VMEM_LIMIT)


def _resident(shape):
    zeros = (0,) * len(shape)
    return pl.BlockSpec(shape, lambda *_: zeros, pipeline_mode=pl.Buffered(1))


def _rms(x, w):
    return x * lax.rsqrt(jnp.mean(x * x, axis=-1, keepdims=True) + RMS_EPS) * w


def _dot(a, b):
    return jnp.dot(a, b, preferred_element_type=F32)


def _cast_kernel(w_ref, o_ref, *, axis, size, tile):
    pos = pl.program_id(0) * tile + lax.broadcasted_iota(jnp.int32, w_ref.shape, axis)
    o_ref[...] = jnp.where(pos < size, w_ref[...], 0.0).astype(BF16)


def _cast_padded(w, axis):
    size = w.shape[axis]
    n = pl.cdiv(size, FF_TILE)
    block = tuple(FF_TILE if a == axis else s for a, s in enumerate(w.shape))
    index = (lambda i: (i, 0)) if axis == 0 else (lambda i: (0, i))
    out_shape = tuple(n * FF_TILE if a == axis else s for a, s in enumerate(w.shape))
    return pl.pallas_call(
        functools.partial(_cast_kernel, axis=axis, size=size, tile=FF_TILE),
        grid=(n,),
        in_specs=[pl.BlockSpec(block, index)],
        out_specs=pl.BlockSpec(block, index),
        out_shape=jax.ShapeDtypeStruct(out_shape, BF16),
        compiler_params=_params("parallel"),
        name="cast",
    )(w)


def _ffn_kernel(x_ref, nw_ref, wg_ref, wu_ref, wd_ref, o_ref, xn_ref, *, nf):
    f = pl.program_id(1)

    @pl.when(f == 0)
    def _():
        xn_ref[...] = _rms(x_ref[...], nw_ref[...]).astype(BF16)
        o_ref[...] = jnp.zeros_like(o_ref)

    xn = xn_ref[...]
    g = _dot(xn, wg_ref[...])
    u = _dot(xn, wu_ref[...])
    h = (jax.nn.silu(g) * u).astype(BF16)
    o_ref[...] += _dot(h, wd_ref[...])

    @pl.when(f == nf - 1)
    def _():
        o_ref[...] = x_ref[...] + 0.5 * o_ref[...]


def _ffn(x, nw, wg, wu, wd, tm):
    m, d = x.shape
    nf = wg.shape[1] // FF_TILE
    return pl.pallas_call(
        functools.partial(_ffn_kernel, nf=nf),
        grid=(m // tm, nf),
        in_specs=[
            pl.BlockSpec((tm, d), lambda i, f: (i, 0)),
            pl.BlockSpec((1, d), lambda i, f: (0, 0)),
            pl.BlockSpec((d, FF_TILE), lambda i, f: (0, f)),
            pl.BlockSpec((d, FF_TILE), lambda i, f: (0, f)),
            pl.BlockSpec((FF_TILE, d), lambda i, f: (f, 0)),
        ],
        out_specs=pl.BlockSpec((tm, d), lambda i, f: (i, 0)),
        out_shape=jax.ShapeDtypeStruct((m, d), F32),
        scratch_shapes=[pltpu.VMEM((tm, d), BF16)],
        compiler_params=_params("parallel", "arbitrary"),
        name="ffn",
    )(x, nw, wg, wu, wd)


def _group_rms(z, gain):
    outs = []
    for h in range(z.shape[1] // HEAD_DIM):
        sl = slice(h * HEAD_DIM, (h + 1) * HEAD_DIM)
        g = gain if gain.shape[1] == HEAD_DIM else gain[:, sl]
        outs.append(_rms(z[:, sl], g))
    return outs


def _proj_kernel(h_ref, nw_ref, w_ref, bf_ref, qn_ref, kn_ref, vmn_ref,
                 q_ref, k_ref, v_ref, lf_ref, u_ref, vm_ref, *attn_refs, q_scale):
    a = _rms(h_ref[...], nw_ref[...]).astype(BF16)
    aw = ATTN_WIDTH

    zq = _dot(a, w_ref[:, 0:aw])
    for h, qh in enumerate(_group_rms(zq, qn_ref[...])):
        q_ref[:, h * HEAD_DIM:(h + 1) * HEAD_DIM] = (qh * q_scale).astype(BF16)

    zk = _dot(a, w_ref[:, aw:2 * aw])
    for h, kh in enumerate(_group_rms(zk, kn_ref[...])):
        sl = slice(h * HEAD_DIM, (h + 1) * HEAD_DIM)
        k_ref[:, sl] = kh
        if attn_refs:
            attn_refs[0][:, sl] = kh.astype(BF16)

    zv = _dot(a, w_ref[:, 2 * aw:3 * aw])
    v_ref[...] = zv
    if attn_refs:
        attn_refs[1][...] = zv.T.astype(BF16)

    zu = _dot(a, w_ref[:, 3 * aw:3 * aw + MLP_WIDTH])
    u_ref[...] = jax.nn.gelu(zu).astype(u_ref.dtype)

    zm = _dot(a, w_ref[:, 3 * aw + MLP_WIDTH:3 * aw + 2 * MLP_WIDTH])
    for g, vg in enumerate(_group_rms(jax.nn.gelu(zm), vmn_ref[...])):
        vm_ref[:, g * HEAD_DIM:(g + 1) * HEAD_DIM] = vg.astype(vm_ref.dtype)

    n_main = 3 * aw + 2 * MLP_WIDTH
    zf = _dot(a, w_ref[:, n_main:n_main + HEAD_DIM]) + bf_ref[...]
    lf_ref[...] = jnp.minimum(zf, 0.0) - jnp.log1p(jnp.exp(-jnp.abs(zf)))


def _proj(h, nw, w_in, b_f, qn, kn, vmn, tm, q_scale, for_attn):
    m, d = h.shape
    row = lambda w: pl.BlockSpec((tm, w), lambda i: (i, 0))
    small = lambda w: pl.BlockSpec((1, w), lambda i: (0, 0))
    mlp_dt = BF16 if for_attn else F32
    out_w = [(ATTN_WIDTH, BF16), (ATTN_WIDTH, F32), (ATTN_WIDTH, F32), (HEAD_DIM, F32),
             (MLP_WIDTH, mlp_dt), (MLP_WIDTH, mlp_dt)]
    out_specs = [row(w) for w, _ in out_w]
    out_shape = [jax.ShapeDtypeStruct((m, w), dt) for w, dt in out_w]
    if for_attn:
        out_specs += [row(ATTN_WIDTH), pl.BlockSpec((ATTN_WIDTH, tm), lambda i: (0, i))]
        out_shape += [jax.ShapeDtypeStruct((m, ATTN_WIDTH), BF16),
                      jax.ShapeDtypeStruct((ATTN_WIDTH, m), BF16)]
    return pl.pallas_call(
        functools.partial(_proj_kernel, q_scale=q_scale),
        grid=(m // tm,),
        in_specs=[row(d), small(d), _resident(w_in.shape),
                  small(HEAD_DIM), small(HEAD_DIM), small(HEAD_DIM), small(MLP_WIDTH)],
        out_specs=out_specs,
        out_shape=out_shape,
        compiler_params=_params("parallel"),
        name="proj",
    )(h, nw, w_in, b_f, qn, kn, vmn)


def _cumsum_kernel(lf_ref, f_ref):
    x = lf_ref[...]
    n = x.shape[0]
    row = lax.broadcasted_iota(jnp.int32, x.shape, 0)
    s = 1
    while s < n:
        x = x + jnp.where(row >= s, pltpu.roll(x, s, 0), 0.0)
        s *= 2
    f_ref[...] = x * LOG2E


def _cumsum(lf, seq):
    return pl.pallas_call(
        _cumsum_kernel,
        grid=(lf.shape[0] // seq,),
        in_specs=[pl.BlockSpec((seq, HEAD_DIM), lambda b: (b, 0))],
        out_specs=pl.BlockSpec((seq, HEAD_DIM), lambda b: (b, 0)),
        out_shape=jax.ShapeDtypeStruct(lf.shape, F32),
        compiler_params=_params("parallel"),
        name="cumsum",
    )(lf)


def _attn_kernel(q_ref, k_ref, vt_ref, f_ref, o_ref, fcol_ref, *, t, nq):
    h = pl.program_id(1)
    key = lax.broadcasted_iota(jnp.int32, (t, t), 0)
    qry = lax.broadcasted_iota(jnp.int32, (t, t), 1)
    nt = (((1,), (1,)), ((), ()))

    f = f_ref[...]
    lane = lax.broadcasted_iota(jnp.int32, f.shape, 1)
    fcol_ref[...] = jnp.sum(jnp.where(lane == h, f, 0.0), axis=-1, keepdims=True)

    def scores(q, lo, hi):
        return (lax.dot_general(k_ref[lo:hi, :], q, nt, preferred_element_type=F32)
                - fcol_ref[lo:hi, :])

    for blk in range(nq):
        past, n = blk * t, (blk + 1) * t
        q = q_ref[past:n, :]
        s_diag = jnp.where(key <= qry, scores(q, past, n), -jnp.inf)
        m = jnp.max(s_diag, axis=0, keepdims=True)
        if past:
            s_past = scores(q, 0, past)
            m = jnp.maximum(m, jnp.max(s_past, axis=0, keepdims=True))
        p = jnp.exp2(s_diag - m)
        l = jnp.sum(p, axis=0, keepdims=True)
        acc = _dot(vt_ref[:, past:n], p.astype(BF16))
        if past:
            p = jnp.exp2(s_past - m)
            l = l + jnp.sum(p, axis=0, keepdims=True)
            acc = acc + _dot(vt_ref[:, 0:past], p.astype(BF16))
        o_ref[past:n, :] = (acc / l).T.astype(o_ref.dtype)


def _attn(q, kb, vt, f, batch, seq):
    t = ATT_TILE
    nq = seq // t
    return pl.pallas_call(
        functools.partial(_attn_kernel, t=t, nq=nq),
        grid=(batch, N_HEADS),
        in_specs=[
            pl.BlockSpec((seq, HEAD_DIM), lambda b, h: (b, h)),
            pl.BlockSpec((seq, HEAD_DIM), lambda b, h: (b, h)),
            pl.BlockSpec((HEAD_DIM, seq), lambda b, h: (h, b)),
            pl.BlockSpec((seq, HEAD_DIM), lambda b, h: (b, 0)),
        ],
        out_specs=pl.BlockSpec((seq, HEAD_DIM), lambda b, h: (b, h)),
        out_shape=jax.ShapeDtypeStruct((batch * seq, ATTN_WIDTH), BF16),
        scratch_shapes=[pltpu.VMEM((seq, 1), F32)],
        compiler_params=_params("parallel", "arbitrary"),
        name="attn",
    )(q, kb, vt, f)


def _mixout_kernel(att_ref, u_ref, vm_ref, sw_ref, sbt_ref, wo_ref, h_ref, o_ref, mlp_ref):
    tm = att_ref.shape[0]
    r = lax.broadcasted_iota(jnp.int32, (CHUNK, CHUNK), 0)
    c = lax.broadcasted_iota(jnp.int32, (CHUNK, CHUNK), 1)
    for g in range(N_GROUPS):
        w = jnp.where(r >= c, sw_ref[g], 0.0).astype(BF16)
        bias = sbt_ref[:, g:g + 1]
        cols = slice(g * HEAD_DIM, (g + 1) * HEAD_DIM)
        for n in range(tm // CHUNK):
            rows = slice(n * CHUNK, (n + 1) * CHUNK)
            mixed = _dot(w, vm_ref[rows, cols].astype(BF16)) + bias
            mlp_ref[rows, cols] = (u_ref[rows, cols] * mixed).astype(BF16)
    o_ref[...] = (h_ref[...] + _dot(att_ref[...], wo_ref[0:ATTN_WIDTH, :])
                  + _dot(mlp_ref[...], wo_ref[ATTN_WIDTH:, :]))


def _mixout(att, u, vm, sw, sbt, wo, h, tm):
    m, d = h.shape
    row = lambda w: pl.BlockSpec((tm, w), lambda i: (i, 0))
    return pl.pallas_call(
        _mixout_kernel,
        grid=(m // tm,),
        in_specs=[row(ATTN_WIDTH), row(MLP_WIDTH), row(MLP_WIDTH), _resident(sw.shape),
                  _resident(sbt.shape), _resident(wo.shape), row(d)],
        out_specs=row(d),
        out_shape=jax.ShapeDtypeStruct((m, d), F32),
        scratch_shapes=[pltpu.VMEM((tm, MLP_WIDTH), BF16)],
        compiler_params=_params("parallel"),
        name="mixout",
    )(att, u, vm, sw, sbt, wo, h)


def _mixout_one_kernel(att_ref, u_ref, vm_ref, w00_ref, b0_ref, wo_ref, h_ref, o_ref):
    mlp = u_ref[...] * (w00_ref[...] * vm_ref[...] + b0_ref[...])
    o_ref[...] = (h_ref[...] + _dot(att_ref[...].astype(BF16), wo_ref[0:ATTN_WIDTH, :])
                  + _dot(mlp.astype(BF16), wo_ref[ATTN_WIDTH:, :]))


def _mixout_one(att, u, vm, w00, b0, wo, h):
    m, d = h.shape
    full = lambda a: pl.BlockSpec(a.shape, lambda i: (0,) * a.ndim)
    args = (att, u, vm, w00, b0, wo, h)
    return pl.pallas_call(
        _mixout_one_kernel,
        grid=(1,),
        in_specs=[full(a) for a in args],
        out_specs=pl.BlockSpec((m, d), lambda i: (0, 0)),
        out_shape=jax.ShapeDtypeStruct((m, d), F32),
        compiler_params=_params("arbitrary"),
        name="mixout_one",
    )(*args)


def _decode_pages(q, state, ck, cv, clf):
    m_old, l_old, acc_old, carry = state
    sub = lax.broadcasted_iota(jnp.int32, (N_HEADS, HEAD_DIM), 0)
    lane = lax.broadcasted_iota(jnp.int32, (N_HEADS, PAGE), 1)

    def head_rows(page_ref, h):
        return page_ref[pl.ds(h, PAGE, stride=N_HEADS), :].astype(BF16)

    def own_head(per_head):
        out = per_head(0)
        for h in range(1, N_HEADS):
            out = jnp.where(sub == h, per_head(h), out)
        return out

    logits = []
    for j in range(len(ck)):
        lf = clf[j][...].T
        y = lf
        s = 1
        while s < PAGE:
            y = y + jnp.where(lane + s < PAGE, pltpu.roll(y, PAGE - s, 1), 0.0)
            s *= 2
        decay = y - lf + carry
        carry = carry + y[:, 0:1]
        scores = own_head(lambda h: lax.dot_general(
            q, head_rows(ck[j], h), (((1,), (1,)), ((), ())),
            preferred_element_type=F32))
        logits.append(scores + decay)

    m_new = m_old
    for lg in logits:
        m_new = jnp.maximum(m_new, jnp.max(lg, axis=-1, keepdims=True))
    alpha = jnp.exp(m_old - m_new)
    l = alpha * l_old
    acc = alpha * acc_old
    for j in range(len(cv)):
        pr = jnp.exp(logits[j] - m_new)
        l = l + jnp.sum(pr, axis=-1, keepdims=True)
        prb = pr.astype(BF16)
        acc = acc + own_head(lambda h: _dot(prb, head_rows(cv[j], h)))
    return m_new, l, acc, carry


def _decode_kernel(pt_ref, q_ref, kn_ref, vn_ref, lfn_ref, *refs, n_steps):
    g = DEC_PAGES
    ck, cv, clf = refs[0:g], refs[g:2 * g], refs[2 * g:3 * g]
    o_ref, m_ref, l_ref, acc_ref, c_ref = refs[3 * g:]
    step = pl.program_id(1)
    q = q_ref[...]

    @pl.when(step == 0)
    def _():
        kn = kn_ref[...].astype(BF16).astype(F32)
        m_ref[...] = jnp.sum(q.astype(F32) * kn, axis=-1, keepdims=True)
        l_ref[...] = jnp.ones_like(l_ref)
        acc_ref[...] = vn_ref[...]
        c_ref[...] = lfn_ref[...]

    state = (m_ref[...], l_ref[...], acc_ref[...], c_ref[...])
    m_ref[...], l_ref[...], acc_ref[...], c_ref[...] = _decode_pages(q, state, ck, cv, clf)

    @pl.when(step == n_steps - 1)
    def _():
        o_ref[...] = acc_ref[...] / l_ref[...]


def _decode(page_table, q, kn, vn, lfn, ck, cv, clf):
    nb, n_pages = page_table.shape
    g = DEC_PAGES
    n_steps = n_pages // g
    per_b = lambda b, s, pt: (b, 0, 0)

    def page(j):
        return lambda b, s, pt: (pt[b, n_pages - 1 - (s * g + j)], 0, 0)

    kv_spec = lambda j: pl.BlockSpec((None, PAGE * N_HEADS, HEAD_DIM), page(j))
    lf_spec = lambda j: pl.BlockSpec((None, PAGE, N_HEADS), page(j))
    head_block = pl.BlockSpec((None, N_HEADS, HEAD_DIM), per_b)
    grid_spec = pltpu.PrefetchScalarGridSpec(
        num_scalar_prefetch=1,
        grid=(nb, n_steps),
        in_specs=([head_block, head_block, head_block, pl.BlockSpec((None, N_HEADS, 1), per_b)]
                  + [kv_spec(j) for j in range(g)] + [kv_spec(j) for j in range(g)]
                  + [lf_spec(j) for j in range(g)]),
        out_specs=head_block,
        scratch_shapes=[pltpu.VMEM((N_HEADS, 1), F32), pltpu.VMEM((N_HEADS, 1), F32),
                        pltpu.VMEM((N_HEADS, HEAD_DIM), F32), pltpu.VMEM((N_HEADS, 1), F32)],
    )
    return pl.pallas_call(
        functools.partial(_decode_kernel, n_steps=n_steps),
        grid_spec=grid_spec,
        out_shape=jax.ShapeDtypeStruct((nb, N_HEADS, HEAD_DIM), F32),
        compiler_params=_params("parallel", "arbitrary"),
        name="decode",
    )(page_table, q, kn, vn, lfn, *([ck] * g), *([cv] * g), *([clf] * g))


def _ple_kernel(h_ref, p_ref, nw_ref, wg_ref, wp_ref, o_ref):
    h = h_ref[...]
    gate = jax.nn.sigmoid(_dot(_rms(h, nw_ref[...]).astype(BF16), wg_ref[...]))
    o_ref[...] = h + gate * _dot(p_ref[...].astype(BF16), wp_ref[...])


def _ple(h, p, nw, wg, wp, tm):
    m, d = h.shape
    return pl.pallas_call(
        _ple_kernel,
        grid=(m // tm,),
        in_specs=[pl.BlockSpec((tm, d), lambda i: (i, 0)),
                  pl.BlockSpec((tm, p.shape[1]), lambda i: (i, 0)),
                  pl.BlockSpec((1, d), lambda i: (0, 0)),
                  _resident(wg.shape), _resident(wp.shape)],
        out_specs=pl.BlockSpec((tm, d), lambda i: (i, 0)),
        out_shape=jax.ShapeDtypeStruct((m, d), F32),
        compiler_params=_params("parallel"),
        name="ple",
    )(h, p, nw, wg, wp)


def _pad_cols(w, mult):
    return jnp.pad(w, ((0, 0), (0, (-w.shape[1]) % mult)))


def kernel(x_prompt, x_sample, p_prompt, p_sample, cache_k, cache_v, cache_logf, page_table,
           ffn1_norm, ffn1_w_gate, ffn1_w_up, ffn1_w_down, mix_norm, w_in, b_f, q_norm, k_norm,
           v_mlp_norm, spatial_w, spatial_b, w_o, ffn2_norm, ffn2_w_gate, ffn2_w_up, ffn2_w_down,
           ple_norm, w_ple_gate, w_ple_proj):
    batch, seq, d = x_prompt.shape
    nb, dec_seq, _ = x_sample.shape
    depth = ffn1_norm.shape[0]
    m = batch * seq
    assert depth == 1 and dec_seq == 1
    assert seq % ATT_TILE == 0 and m % ROW_TILE == 0 and m % PROJ_TILE == 0
    assert page_table.shape[1] % DEC_PAGES == 0
    ffn_tm = FFN_ROW_TILE if m % FFN_ROW_TILE == 0 else ROW_TILE
    l = 0

    def ffn_weights(wg, wu, wd):
        return _cast_padded(wg, 1), _cast_padded(wu, 1), _cast_padded(wd, 0)

    ffn1 = (ffn1_norm[l][None],) + ffn_weights(ffn1_w_gate[l], ffn1_w_up[l], ffn1_w_down[l])
    ffn2 = (ffn2_norm[l][None],) + ffn_weights(ffn2_w_gate[l], ffn2_w_up[l], ffn2_w_down[l])
    proj = (mix_norm[l][None], _pad_cols(w_in[l].astype(BF16), FF_TILE),
            _pad_cols(b_f[l][None], HEAD_DIM), q_norm[l][None], k_norm[l][None],
            v_mlp_norm[l].reshape(1, MLP_WIDTH))
    wo = w_o[l].astype(BF16)
    ple = (ple_norm[l][None], w_ple_gate[l].astype(BF16), w_ple_proj[l].astype(BF16))
    heads = (N_HEADS, HEAD_DIM)

    hp = _ffn(x_prompt.reshape(m, d), *ffn1, tm=ffn_tm)
    q, k, v, lf, u, vm, kb, vt = _proj(hp, *proj, tm=PROJ_TILE, q_scale=QK_SCALE * LOG2E,
                                       for_attn=True)
    att = _attn(q, kb, vt, _cumsum(lf, seq), batch, seq)
    hp = _mixout(att, u, vm, spatial_w[l], spatial_b[l].T, wo, hp, tm=ROW_TILE)
    hp = _ffn(hp, *ffn2, tm=ffn_tm)
    hp = _ple(hp, p_prompt[l].reshape(m, -1), *ple, tm=ROW_TILE)

    hs = _ffn(x_sample.reshape(nb, d), *ffn1, tm=nb)
    qs, ks, vs, lfs, us, vms = _proj(hs, *proj, tm=nb, q_scale=QK_SCALE, for_attn=False)
    att_s = _decode(page_table, qs.reshape(nb, *heads), ks.reshape(nb, *heads),
                    vs.reshape(nb, *heads), lfs[:, :N_HEADS].reshape(nb, N_HEADS, 1),
                    cache_k[l].reshape(-1, PAGE * N_HEADS, HEAD_DIM),
                    cache_v[l].reshape(-1, PAGE * N_HEADS, HEAD_DIM),
                    cache_logf[l])
    w00 = jnp.repeat(spatial_w[l][:, 0, 0], HEAD_DIM)[None]
    b0 = jnp.repeat(spatial_b[l][:, 0], HEAD_DIM)[None]
    hs = _mixout_one(att_s.reshape(nb, ATTN_WIDTH), us, vms, w00, b0, wo, hs)
    hs = _ffn(hs, *ffn2, tm=nb)
    hs = _ple(hs, p_sample[l].reshape(nb, -1), *ple, tm=nb)

    return (hp.reshape(batch, seq, d), hs.reshape(nb, 1, d),
            k.reshape(1, batch, seq, *heads), v.reshape(1, batch, seq, *heads),
            lf[:, :N_HEADS].reshape(1, batch, seq, N_HEADS),
            ks.reshape(1, nb, 1, *heads), vs.reshape(1, nb, 1, *heads),
            lfs[:, :N_HEADS].reshape(1, nb, 1, N_HEADS),
            vms.reshape(1, nb, 1, N_GROUPS, HEAD_DIM))
```

```python
import functools

import jax
import jax.numpy as jnp
from jax import lax
from jax.experimental import pallas as pl
from jax.experimental.pallas import tpu as pltpu

F32 = jnp.float32
BF16 = jnp.bfloat16

HEAD_DIM = 128
N_HEADS = 8
N_GROUPS = 8
ATTN_WIDTH = N_HEADS * HEAD_DIM
MLP_WIDTH = N_GROUPS * HEAD_DIM
CHUNK = 128
PAGE = 128
RMS_EPS = 1e-6
QK_SCALE = HEAD_DIM ** -0.5
LOG2E = 1.4426950408889634

V7X_VMEM_BYTES = 64 * 1024 * 1024
VMEM_LIMIT = V7X_VMEM_BYTES - 8 * 1024 * 1024

FF_TILE = 512
FFN_ROW_TILE = 1024
ROW_TILE = 512
PROJ_TILE = 512
ATT_TILE = 512
DEC_PAGES = 16


def _params(*sem):
    return pltpu.CompilerParams(dimension_semantics=sem, vmem_limit_bytes=VMEM_LIMIT)


def _resident(shape):
    zeros = (0,) * len(shape)
    return pl.BlockSpec(shape, lambda *_: zeros, pipeline_mode=pl.Buffered(1))


def _rms(x, w):
    return x * lax.rsqrt(jnp.mean(x * x, axis=-1, keepdims=True) + RMS_EPS) * w


def _dot(a, b):
    return jnp.dot(a, b, preferred_element_type=F32)


def _cast_kernel(w_ref, o_ref, *, axis, size, tile):
    pos = pl.program_id(0) * tile + lax.broadcasted_iota(jnp.int32, w_ref.shape, axis)
    o_ref[...] = jnp.where(pos < size, w_ref[...], 0.0).astype(BF16)


def _cast_padded(w, axis):
    size = w.shape[axis]
    n = pl.cdiv(size, FF_TILE)
    block = tuple(FF_TILE if a == axis else s for a, s in enumerate(w.shape))
    index = (lambda i: (i, 0)) if axis == 0 else (lambda i: (0, i))
    out_shape = tuple(n * FF_TILE if a == axis else s for a, s in enumerate(w.shape))
    return pl.pallas_call(
        functools.partial(_cast_kernel, axis=axis, size=size, tile=FF_TILE),
        grid=(n,),
        in_specs=[pl.BlockSpec(block, index)],
        out_specs=pl.BlockSpec(block, index),
        out_shape=jax.ShapeDtypeStruct(out_shape, BF16),
        compiler_params=_params("parallel"),
        name="cast",
    )(w)


def _ffn_kernel(x_ref, xs_ref, nw_ref, wg_ref, wu_ref, wd_ref, o_ref, os_ref, xn_ref, xsn_ref,
                *, nf, n_tiles):
    i = pl.program_id(0)
    f = pl.program_id(1)

    def step(x_ref, o_ref, xn_ref):
        @pl.when(f == 0)
        def _():
            xn_ref[...] = _rms(x_ref[...], nw_ref[...]).astype(BF16)
            o_ref[...] = jnp.zeros_like(o_ref)

        xn = xn_ref[...]
        g = _dot(xn, wg_ref[...])
        u = _dot(xn, wu_ref[...])
        h = (jax.nn.silu(g) * u).astype(BF16)
        o_ref[...] += _dot(h, wd_ref[...])

        @pl.when(f == nf - 1)
        def _():
            o_ref[...] = x_ref[...] + 0.5 * o_ref[...]

    pl.when(i < n_tiles)(functools.partial(step, x_ref, o_ref, xn_ref))
    pl.when(i == n_tiles)(functools.partial(step, xs_ref, os_ref, xsn_ref))


def _ffn(x, xs, nw, wg, wu, wd, tm):
    m, d = x.shape
    ms = xs.shape[0]
    nf = wg.shape[1] // FF_TILE
    n_tiles = m // tm
    tile = lambda i, f: (jnp.minimum(i, n_tiles - 1), 0)
    return pl.pallas_call(
        functools.partial(_ffn_kernel, nf=nf, n_tiles=n_tiles),
        grid=(n_tiles + 1, nf),
        in_specs=[
            pl.BlockSpec((tm, d), tile),
            pl.BlockSpec((ms, d), lambda i, f: (0, 0)),
            pl.BlockSpec((1, d), lambda i, f: (0, 0)),
            pl.BlockSpec((d, FF_TILE), lambda i, f: (0, f)),
            pl.BlockSpec((d, FF_TILE), lambda i, f: (0, f)),
            pl.BlockSpec((FF_TILE, d), lambda i, f: (f, 0)),
        ],
        out_specs=[pl.BlockSpec((tm, d), tile), pl.BlockSpec((ms, d), lambda i, f: (0, 0))],
        out_shape=[jax.ShapeDtypeStruct((m, d), F32), jax.ShapeDtypeStruct((ms, d), F32)],
        scratch_shapes=[pltpu.VMEM((tm, d), BF16), pltpu.VMEM((ms, d), BF16)],
        compiler_params=_params("arbitrary", "arbitrary"),
        name="ffn",
    )(x, xs, nw, wg, wu, wd)


def _group_rms(z, gain):
    outs = []
    for h in range(z.shape[1] // HEAD_DIM):
        sl = slice(h * HEAD_DIM, (h + 1) * HEAD_DIM)
        g = gain if gain.shape[1] == HEAD_DIM else gain[:, sl]
        outs.append(_rms(z[:, sl], g))
    return outs


def _proj_kernel(h_ref, nw_ref, w_ref, bf_ref, qn_ref, kn_ref, vmn_ref,
                 q_ref, k_ref, v_ref, lf_ref, u_ref, vm_ref, *attn_refs, q_scale):
    a = _rms(h_ref[...], nw_ref[...]).astype(BF16)
    aw = ATTN_WIDTH

    zq = _dot(a, w_ref[:, 0:aw])
    for h, qh in enumerate(_group_rms(zq, qn_ref[...])):
        q_ref[:, h * HEAD_DIM:(h + 1) * HEAD_DIM] = (qh * q_scale).astype(BF16)

    zk = _dot(a, w_ref[:, aw:2 * aw])
    for h, kh in enumerate(_group_rms(zk, kn_ref[...])):
        sl = slice(h * HEAD_DIM, (h + 1) * HEAD_DIM)
        k_ref[:, sl] = kh
        if attn_refs:
            attn_refs[0][:, sl] = kh.astype(BF16)

    zv = _dot(a, w_ref[:, 2 * aw:3 * aw])
    v_ref[...] = zv
    if attn_refs:
        attn_refs[1][...] = zv.T.astype(BF16)

    zu = _dot(a, w_ref[:, 3 * aw:3 * aw + MLP_WIDTH])
    u_ref[...] = jax.nn.gelu(zu).astype(u_ref.dtype)

    zm = _dot(a, w_ref[:, 3 * aw + MLP_WIDTH:3 * aw + 2 * MLP_WIDTH])
    for g, vg in enumerate(_group_rms(jax.nn.gelu(zm), vmn_ref[...])):
        vm_ref[:, g * HEAD_DIM:(g + 1) * HEAD_DIM] = vg.astype(vm_ref.dtype)

    n_main = 3 * aw + 2 * MLP_WIDTH
    zf = _dot(a, w_ref[:, n_main:n_main + HEAD_DIM]) + bf_ref[...]
    lf_ref[...] = jnp.minimum(zf, 0.0) - jnp.log1p(jnp.exp(-jnp.abs(zf)))


def _proj(h, nw, w_in, b_f, qn, kn, vmn, tm, q_scale, for_attn):
    m, d = h.shape
    row = lambda w: pl.BlockSpec((tm, w), lambda i: (i, 0))
    small = lambda w: pl.BlockSpec((1, w), lambda i: (0, 0))
    mlp_dt = BF16 if for_attn else F32
    out_w = [(ATTN_WIDTH, BF16), (ATTN_WIDTH, F32), (ATTN_WIDTH, F32), (HEAD_DIM, F32),
             (MLP_WIDTH, mlp_dt), (MLP_WIDTH, mlp_dt)]
    out_specs = [row(w) for w, _ in out_w]
    out_shape = [jax.ShapeDtypeStruct((m, w), dt) for w, dt in out_w]
    if for_attn:
        out_specs += [row(ATTN_WIDTH), pl.BlockSpec((ATTN_WIDTH, tm), lambda i: (0, i))]
        out_shape += [jax.ShapeDtypeStruct((m, ATTN_WIDTH), BF16),
                      jax.ShapeDtypeStruct((ATTN_WIDTH, m), BF16)]
    return pl.pallas_call(
        functools.partial(_proj_kernel, q_scale=q_scale),
        grid=(m // tm,),
        in_specs=[row(d), small(d), _resident(w_in.shape),
                  small(HEAD_DIM), small(HEAD_DIM), small(HEAD_DIM), small(MLP_WIDTH)],
        out_specs=out_specs,
        out_shape=out_shape,
        compiler_params=_params("parallel"),
        name="proj",
    )(h, nw, w_in, b_f, qn, kn, vmn)


def _cumsum_kernel(lf_ref, f_ref):
    x = lf_ref[...]
    n = x.shape[0]
    row = lax.broadcasted_iota(jnp.int32, x.shape, 0)
    s = 1
    while s < n:
        x = x + jnp.where(row >= s, pltpu.roll(x, s, 0), 0.0)
        s *= 2
    f_ref[...] = x * LOG2E


def _cumsum(lf, seq):
    return pl.pallas_call(
        _cumsum_kernel,
        grid=(lf.shape[0] // seq,),
        in_specs=[pl.BlockSpec((seq, HEAD_DIM), lambda b: (b, 0))],
        out_specs=pl.BlockSpec((seq, HEAD_DIM), lambda b: (b, 0)),
        out_shape=jax.ShapeDtypeStruct(lf.shape, F32),
        compiler_params=_params("parallel"),
        name="cumsum",
    )(lf)


def _attn_kernel(q_ref, k_ref, vt_ref, f_ref, o_ref, fcol_ref, *, t, nq):
    h = pl.program_id(1)
    key = lax.broadcasted_iota(jnp.int32, (t, t), 0)
    qry = lax.broadcasted_iota(jnp.int32, (t, t), 1)
    nt = (((1,), (1,)), ((), ()))

    f = f_ref[...]
    lane = lax.broadcasted_iota(jnp.int32, f.shape, 1)
    fcol_ref[...] = jnp.sum(jnp.where(lane == h, f, 0.0), axis=-1, keepdims=True)

    def scores(q, lo, hi):
        return (lax.dot_general(k_ref[lo:hi, :], q, nt, preferred_element_type=F32)
                - fcol_ref[lo:hi, :])

    for blk in range(nq):
        past, n = blk * t, (blk + 1) * t
        q = q_ref[past:n, :]
        s_diag = jnp.where(key <= qry, scores(q, past, n), -jnp.inf)
        m = jnp.max(s_diag, axis=0, keepdims=True)
        if past:
            s_past = scores(q, 0, past)
            m = jnp.maximum(m, jnp.max(s_past, axis=0, keepdims=True))
        p = jnp.exp2(s_diag - m)
        l = jnp.sum(p, axis=0, keepdims=True)
        acc = _dot(vt_ref[:, past:n], p.astype(BF16))
        if past:
            p = jnp.exp2(s_past - m)
            l = l + jnp.sum(p, axis=0, keepdims=True)
            acc = acc + _dot(vt_ref[:, 0:past], p.astype(BF16))
        o_ref[past:n, :] = (acc / l).T.astype(o_ref.dtype)


def _attn(q, kb, vt, f, batch, seq):
    t = ATT_TILE
    nq = seq // t
    return pl.pallas_call(
        functools.partial(_attn_kernel, t=t, nq=nq),
        grid=(batch, N_HEADS),
        in_specs=[
            pl.BlockSpec((seq, HEAD_DIM), lambda b, h: (b, h)),
            pl.BlockSpec((seq, HEAD_DIM), lambda b, h: (b, h)),
            pl.BlockSpec((HEAD_DIM, seq), lambda b, h: (h, b)),
            pl.BlockSpec((seq, HEAD_DIM), lambda b, h: (b, 0)),
        ],
        out_specs=pl.BlockSpec((seq, HEAD_DIM), lambda b, h: (b, h)),
        out_shape=jax.ShapeDtypeStruct((batch * seq, ATTN_WIDTH), BF16),
        scratch_shapes=[pltpu.VMEM((seq, 1), F32)],
        compiler_params=_params("parallel", "arbitrary"),
        name="attn",
    )(q, kb, vt, f)


def _mixout_kernel(att_ref, u_ref, vm_ref, sw_ref, sbt_ref, wo_ref, h_ref, o_ref, mlp_ref):
    tm = att_ref.shape[0]
    r = lax.broadcasted_iota(jnp.int32, (CHUNK, CHUNK), 0)
    c = lax.broadcasted_iota(jnp.int32, (CHUNK, CHUNK), 1)
    for g in range(N_GROUPS):
        w = jnp.where(r >= c, sw_ref[g], 0.0).astype(BF16)
        bias = sbt_ref[:, g:g + 1]
        cols = slice(g * HEAD_DIM, (g + 1) * HEAD_DIM)
        for n in range(tm // CHUNK):
            rows = slice(n * CHUNK, (n + 1) * CHUNK)
            mixed = _dot(w, vm_ref[rows, cols].astype(BF16)) + bias
            mlp_ref[rows, cols] = (u_ref[rows, cols] * mixed).astype(BF16)
    o_ref[...] = (h_ref[...] + _dot(att_ref[...], wo_ref[0:ATTN_WIDTH, :])
                  + _dot(mlp_ref[...], wo_ref[ATTN_WIDTH:, :]))


def _mixout(att, u, vm, sw, sbt, wo, h, tm):
    m, d = h.shape
    row = lambda w: pl.BlockSpec((tm, w), lambda i: (i, 0))
    return pl.pallas_call(
        _mixout_kernel,
        grid=(m // tm,),
        in_specs=[row(ATTN_WIDTH), row(MLP_WIDTH), row(MLP_WIDTH), _resident(sw.shape),
                  _resident(sbt.shape), _resident(wo.shape), row(d)],
        out_specs=row(d),
        out_shape=jax.ShapeDtypeStruct((m, d), F32),
        scratch_shapes=[pltpu.VMEM((tm, MLP_WIDTH), BF16)],
        compiler_params=_params("parallel"),
        name="mixout",
    )(att, u, vm, sw, sbt, wo, h)


def _mixout_one_kernel(att_ref, u_ref, vm_ref, w00_ref, b0_ref, wo_ref, h_ref, o_ref):
    mlp = u_ref[...] * (w00_ref[...] * vm_ref[...] + b0_ref[...])
    o_ref[...] = (h_ref[...] + _dot(att_ref[...].astype(BF16), wo_ref[0:ATTN_WIDTH, :])
                  + _dot(mlp.astype(BF16), wo_ref[ATTN_WIDTH:, :]))


def _mixout_one(att, u, vm, w00, b0, wo, h):
    m, d = h.shape
    full = lambda a: pl.BlockSpec(a.shape, lambda i: (0,) * a.ndim)
    args = (att, u, vm, w00, b0, wo, h)
    return pl.pallas_call(
        _mixout_one_kernel,
        grid=(1,),
        in_specs=[full(a) for a in args],
        out_specs=pl.BlockSpec((m, d), lambda i: (0, 0)),
        out_shape=jax.ShapeDtypeStruct((m, d), F32),
        compiler_params=_params("arbitrary"),
        name="mixout_one",
    )(*args)


def _decode_pages(q, state, ck, cv, clf):
    m_old, l_old, acc_old, carry = state
    sub = lax.broadcasted_iota(jnp.int32, (N_HEADS, HEAD_DIM), 0)
    lane = lax.broadcasted_iota(jnp.int32, (N_HEADS, PAGE), 1)

    def head_rows(page_ref, h):
        return page_ref[pl.ds(h, PAGE, stride=N_HEADS), :].astype(BF16)

    def own_head(per_head):
        out = per_head(0)
        for h in range(1, N_HEADS):
            out = jnp.where(sub == h, per_head(h), out)
        return out

    logits = []
    for j in range(len(ck)):
        lf = clf[j][...]
        y = lf
        s = 1
        while s < PAGE:
            y = y + jnp.where(lane + s < PAGE, pltpu.roll(y, PAGE - s, 1), 0.0)
            s *= 2
        decay = y - lf + carry
        carry = carry + y[:, 0:1]
        scores = own_head(lambda h: lax.dot_general(
            q, head_rows(ck[j], h), (((1,), (1,)), ((), ())),
            preferred_element_type=F32))
        logits.append(scores + decay)

    m_new = m_old
    for lg in logits:
        m_new = jnp.maximum(m_new, jnp.max(lg, axis=-1, keepdims=True))
    alpha = jnp.exp(m_old - m_new)
    l = alpha * l_old
    acc = alpha * acc_old
    for j in range(len(cv)):
        pr = jnp.exp(logits[j] - m_new)
        l = l + jnp.sum(pr, axis=-1, keepdims=True)
        prb = pr.astype(BF16)
        acc = acc + own_head(lambda h: _dot(prb, head_rows(cv[j], h)))
    return m_new, l, acc, carry


def _decode_kernel(pt_ref, q_ref, kn_ref, vn_ref, lfn_ref, *refs, n_steps):
    g = DEC_PAGES
    ck, cv, clf = refs[0:g], refs[g:2 * g], refs[2 * g:3 * g]
    o_ref, m_ref, l_ref, acc_ref, c_ref = refs[3 * g:]
    step = pl.program_id(1)
    q = q_ref[...]

    @pl.when(step == 0)
    def _():
        kn = kn_ref[...].astype(BF16).astype(F32)
        m_ref[...] = jnp.sum(q.astype(F32) * kn, axis=-1, keepdims=True)
        l_ref[...] = jnp.ones_like(l_ref)
        acc_ref[...] = vn_ref[...]
        c_ref[...] = lfn_ref[...]

    state = (m_ref[...], l_ref[...], acc_ref[...], c_ref[...])
    m_ref[...], l_ref[...], acc_ref[...], c_ref[...] = _decode_pages(q, state, ck, cv, clf)

    @pl.when(step == n_steps - 1)
    def _():
        o_ref[...] = acc_ref[...] / l_ref[...]


def _decode(page_table, q, kn, vn, lfn, ck, cv, clft):
    nb, n_pages = page_table.shape
    g = DEC_PAGES
    n_steps = n_pages // g
    per_b = lambda b, s, pt: (b, 0, 0)

    def page(j):
        return lambda b, s, pt: (pt[b, n_pages - 1 - (s * g + j)], 0, 0)

    kv_spec = lambda j: pl.BlockSpec((None, PAGE * N_HEADS, HEAD_DIM), page(j))
    lf_spec = lambda j: pl.BlockSpec((None, N_HEADS, PAGE), page(j))
    head_block = pl.BlockSpec((None, N_HEADS, HEAD_DIM), per_b)
    grid_spec = pltpu.PrefetchScalarGridSpec(
        num_scalar_prefetch=1,
        grid=(nb, n_steps),
        in_specs=([head_block, head_block, head_block, pl.BlockSpec((None, N_HEADS, 1), per_b)]
                  + [kv_spec(j) for j in range(g)] + [kv_spec(j) for j in range(g)]
                  + [lf_spec(j) for j in range(g)]),
        out_specs=head_block,
        scratch_shapes=[pltpu.VMEM((N_HEADS, 1), F32), pltpu.VMEM((N_HEADS, 1), F32),
                        pltpu.VMEM((N_HEADS, HEAD_DIM), F32), pltpu.VMEM((N_HEADS, 1), F32)],
    )
    return pl.pallas_call(
        functools.partial(_decode_kernel, n_steps=n_steps),
        grid_spec=grid_spec,
        out_shape=jax.ShapeDtypeStruct((nb, N_HEADS, HEAD_DIM), F32),
        compiler_params=_params("parallel", "arbitrary"),
        name="decode",
    )(page_table, q, kn, vn, lfn, *([ck] * g), *([cv] * g), *([clft] * g))


def _ple_kernel(h_ref, p_ref, nw_ref, wg_ref, wp_ref, o_ref):
    h = h_ref[...]
    gate = jax.nn.sigmoid(_dot(_rms(h, nw_ref[...]).astype(BF16), wg_ref[...]))
    o_ref[...] = h + gate * _dot(p_ref[...].astype(BF16), wp_ref[...])


def _ple(h, p, nw, wg, wp, tm):
    m, d = h.shape
    return pl.pallas_call(
        _ple_kernel,
        grid=(m // tm,),
        in_specs=[pl.BlockSpec((tm, d), lambda i: (i, 0)),
                  pl.BlockSpec((tm, p.shape[1]), lambda i: (i, 0)),
                  pl.BlockSpec((1, d), lambda i: (0, 0)),
                  _resident(wg.shape), _resident(wp.shape)],
        out_specs=pl.BlockSpec((tm, d), lambda i: (i, 0)),
        out_shape=jax.ShapeDtypeStruct((m, d), F32),
        compiler_params=_params("parallel"),
        name="ple",
    )(h, p, nw, wg, wp)


def _pad_cols(w, mult):
    return jnp.pad(w, ((0, 0), (0, (-w.shape[1]) % mult)))


def kernel(x_prompt, x_sample, p_prompt, p_sample, cache_k, cache_v, cache_logf, page_table,
           ffn1_norm, ffn1_w_gate, ffn1_w_up, ffn1_w_down, mix_norm, w_in, b_f, q_norm, k_norm,
           v_mlp_norm, spatial_w, spatial_b, w_o, ffn2_norm, ffn2_w_gate, ffn2_w_up, ffn2_w_down,
           ple_norm, w_ple_gate, w_ple_proj):
    batch, seq, d = x_prompt.shape
    nb, dec_seq, _ = x_sample.shape
    depth = ffn1_norm.shape[0]
    m = batch * seq
    assert depth == 1 and dec_seq == 1
    assert seq % ATT_TILE == 0 and m % ROW_TILE == 0 and m % PROJ_TILE == 0
    assert page_table.shape[1] % DEC_PAGES == 0
    ffn_tm = FFN_ROW_TILE if m % FFN_ROW_TILE == 0 else ROW_TILE
    l = 0

    def ffn_weights(wg, wu, wd):
        return _cast_padded(wg, 1), _cast_padded(wu, 1), _cast_padded(wd, 0)

    ffn1 = (ffn1_norm[l][None],) + ffn_weights(ffn1_w_gate[l], ffn1_w_up[l], ffn1_w_down[l])
    ffn2 = (ffn2_norm[l][None],) + ffn_weights(ffn2_w_gate[l], ffn2_w_up[l], ffn2_w_down[l])
    proj = (mix_norm[l][None], _cast_padded(w_in[l], 1),
            _pad_cols(b_f[l][None], HEAD_DIM), q_norm[l][None], k_norm[l][None],
            v_mlp_norm[l].reshape(1, MLP_WIDTH))
    wo = w_o[l].astype(BF16)
    ple = (ple_norm[l][None], w_ple_gate[l].astype(BF16), w_ple_proj[l].astype(BF16))
    heads = (N_HEADS, HEAD_DIM)

    hp, hs = _ffn(x_prompt.reshape(m, d), x_sample.reshape(nb, d), *ffn1, tm=ffn_tm)

    q, k, v, lf, u, vm, kb, vt = _proj(hp, *proj, tm=PROJ_TILE, q_scale=QK_SCALE * LOG2E,
                                       for_attn=True)
    att = _attn(q, kb, vt, _cumsum(lf, seq), batch, seq)
    hp = _mixout(att, u, vm, spatial_w[l], spatial_b[l].T, wo, hp, tm=ROW_TILE)

    qs, ks, vs, lfs, us, vms = _proj(hs, *proj, tm=nb, q_scale=QK_SCALE, for_attn=False)
    att_s = _decode(page_table, qs.reshape(nb, *heads), ks.reshape(nb, *heads),
                    vs.reshape(nb, *heads), lfs[:, :N_HEADS].reshape(nb, N_HEADS, 1),
                    cache_k[l].reshape(-1, PAGE * N_HEADS, HEAD_DIM),
                    cache_v[l].reshape(-1, PAGE * N_HEADS, HEAD_DIM),
                    jnp.swapaxes(cache_logf[l], 1, 2))
    w00 = jnp.repeat(spatial_w[l][:, 0, 0], HEAD_DIM)[None]
    b0 = jnp.repeat(spatial_b[l][:, 0], HEAD_DIM)[None]
    hs = _mixout_one(att_s.reshape(nb, ATTN_WIDTH), us, vms, w00, b0, wo, hs)

    hp, hs = _ffn(hp, hs, *ffn2, tm=ffn_tm)
    hp = _ple(hp, p_prompt[l].reshape(m, -1), *ple, tm=ROW_TILE)
    hs = _ple(hs, p_sample[l].reshape(nb, -1), *ple, tm=nb)

    return (hp.reshape(batch, seq, d), hs.reshape(nb, 1, d),
            k.reshape(1, batch, seq, *heads), v.reshape(1, batch, seq, *heads),
            lf[:, :N_HEADS].reshape(1, batch, seq, N_HEADS),
            ks.reshape(1, nb, 1, *heads), vs.reshape(1, nb, 1, *heads),
            lfs[:, :N_HEADS].reshape(1, nb, 1, N_HEADS),
            vms.reshape(1, nb, 1, N_GROUPS, HEAD_DIM))
```

```python
import functools

import jax
import jax.numpy as jnp
from jax import lax
from jax.experimental import pallas as pl
from jax.experimental.pallas import tpu as pltpu

F32 = jnp.float32
BF16 = jnp.bfloat16

HEAD_DIM = 128
N_HEADS = 8
N_GROUPS = 8
ATTN_WIDTH = N_HEADS * HEAD_DIM
MLP_WIDTH = N_GROUPS * HEAD_DIM
CHUNK = 128
PAGE = 128
RMS_EPS = 1e-6
QK_SCALE = HEAD_DIM ** -0.5
LOG2E = 1.4426950408889634

V7X_VMEM_BYTES = 64 * 1024 * 1024
VMEM_LIMIT = V7X_VMEM_BYTES - 8 * 1024 * 1024

FF_TILE = 512
FFN_ROW_TILE = 1024
ROW_TILE = 512
PROJ_TILE = 512
ATT_TILE = 512
DEC_PAGES = 16


def _params(*sem):
    return pltpu.CompilerParams(dimension_semantics=sem, vmem_limit_bytes=VMEM_LIMIT)


def _resident(shape):
    zeros = (0,) * len(shape)
    return pl.BlockSpec(shape, lambda *_: zeros, pipeline_mode=pl.Buffered(1))


def _rms(x, w):
    return x * lax.rsqrt(jnp.mean(x * x, axis=-1, keepdims=True) + RMS_EPS) * w


def _dot(a, b):
    return jnp.dot(a, b, preferred_element_type=F32)


def _cast_kernel(w_ref, o_ref, *, axis, size, tile):
    pos = pl.program_id(0) * tile + lax.broadcasted_iota(jnp.int32, w_ref.shape, axis)
    o_ref[...] = jnp.where(pos < size, w_ref[...], 0.0).astype(BF16)


def _cast_padded(w, axis):
    size = w.shape[axis]
    n = pl.cdiv(size, FF_TILE)
    block = tuple(FF_TILE if a == axis else s for a, s in enumerate(w.shape))
    index = (lambda i: (i, 0)) if axis == 0 else (lambda i: (0, i))
    out_shape = tuple(n * FF_TILE if a == axis else s for a, s in enumerate(w.shape))
    return pl.pallas_call(
        functools.partial(_cast_kernel, axis=axis, size=size, tile=FF_TILE),
        grid=(n,),
        in_specs=[pl.BlockSpec(block, index)],
        out_specs=pl.BlockSpec(block, index),
        out_shape=jax.ShapeDtypeStruct(out_shape, BF16),
        compiler_params=_params("parallel"),
        name="cast",
    )(w)


def _ffn_kernel(x_ref, nw_ref, wg_ref, wu_ref, wd_ref, o_ref, xn_ref, *, nf):
    f = pl.program_id(1)

    def down(xn):
        g = _dot(xn, wg_ref[...])
        u = _dot(xn, wu_ref[...])
        h = (jax.nn.silu(g) * u).astype(BF16)
        return _dot(h, wd_ref[...])

    @pl.when(f == 0)
    def _():
        xn = _rms(x_ref[...], nw_ref[...]).astype(BF16)
        xn_ref[...] = xn
        o_ref[...] = down(xn)

    @pl.when(f > 0)
    def _():
        o_ref[...] += down(xn_ref[...])

    @pl.when(f == nf - 1)
    def _():
        o_ref[...] = x_ref[...] + 0.5 * o_ref[...]


def _ffn(x, nw, wg, wu, wd, tm):
    m, d = x.shape
    nf = wg.shape[1] // FF_TILE
    return pl.pallas_call(
        functools.partial(_ffn_kernel, nf=nf),
        grid=(m // tm, nf),
        in_specs=[
            pl.BlockSpec((tm, d), lambda i, f: (i, 0)),
            pl.BlockSpec((1, d), lambda i, f: (0, 0)),
            pl.BlockSpec((d, FF_TILE), lambda i, f: (0, f)),
            pl.BlockSpec((d, FF_TILE), lambda i, f: (0, f)),
            pl.BlockSpec((FF_TILE, d), lambda i, f: (f, 0)),
        ],
        out_specs=pl.BlockSpec((tm, d), lambda i, f: (i, 0)),
        out_shape=jax.ShapeDtypeStruct((m, d), F32),
        scratch_shapes=[pltpu.VMEM((tm, d), BF16)],
        compiler_params=_params("parallel", "arbitrary"),
        name="ffn",
    )(x, nw, wg, wu, wd)


def _group_rms(z, gain):
    outs = []
    for h in range(z.shape[1] // HEAD_DIM):
        sl = slice(h * HEAD_DIM, (h + 1) * HEAD_DIM)
        g = gain if gain.shape[1] == HEAD_DIM else gain[:, sl]
        outs.append(_rms(z[:, sl], g))
    return outs


def _proj_kernel(h_ref, nw_ref, w_ref, bf_ref, qn_ref, kn_ref, vmn_ref,
                 q_ref, k_ref, v_ref, lf_ref, u_ref, vm_ref, *attn_refs, q_scale):
    a = _rms(h_ref[...], nw_ref[...]).astype(BF16)
    aw = ATTN_WIDTH

    zq = _dot(a, w_ref[:, 0:aw])
    for h, qh in enumerate(_group_rms(zq, qn_ref[...])):
        q_ref[:, h * HEAD_DIM:(h + 1) * HEAD_DIM] = (qh * q_scale).astype(BF16)

    zk = _dot(a, w_ref[:, aw:2 * aw])
    for h, kh in enumerate(_group_rms(zk, kn_ref[...])):
        sl = slice(h * HEAD_DIM, (h + 1) * HEAD_DIM)
        k_ref[:, sl] = kh
        if attn_refs:
            attn_refs[0][:, sl] = kh.astype(BF16)

    zv = _dot(a, w_ref[:, 2 * aw:3 * aw])
    v_ref[...] = zv
    if attn_refs:
        attn_refs[1][...] = zv.T.astype(BF16)

    zu = _dot(a, w_ref[:, 3 * aw:3 * aw + MLP_WIDTH])
    u_ref[...] = jax.nn.gelu(zu).astype(u_ref.dtype)

    zm = _dot(a, w_ref[:, 3 * aw + MLP_WIDTH:3 * aw + 2 * MLP_WIDTH])
    for g, vg in enumerate(_group_rms(jax.nn.gelu(zm), vmn_ref[...])):
        vm_ref[:, g * HEAD_DIM:(g + 1) * HEAD_DIM] = vg.astype(vm_ref.dtype)

    n_main = 3 * aw + 2 * MLP_WIDTH
    zf = _dot(a, w_ref[:, n_main:n_main + HEAD_DIM]) + bf_ref[...]
    lf_ref[...] = jnp.minimum(zf, 0.0) - jnp.log1p(jnp.exp(-jnp.abs(zf)))


def _proj(h, nw, w_in, b_f, qn, kn, vmn, tm, q_scale, for_attn):
    m, d = h.shape
    row = lambda w: pl.BlockSpec((tm, w), lambda i: (i, 0))
    small = lambda w: pl.BlockSpec((1, w), lambda i: (0, 0))
    mlp_dt = BF16 if for_attn else F32
    out_w = [(ATTN_WIDTH, BF16), (ATTN_WIDTH, F32), (ATTN_WIDTH, F32), (HEAD_DIM, F32),
             (MLP_WIDTH, mlp_dt), (MLP_WIDTH, mlp_dt)]
    out_specs = [row(w) for w, _ in out_w]
    out_shape = [jax.ShapeDtypeStruct((m, w), dt) for w, dt in out_w]
    if for_attn:
        out_specs += [row(ATTN_WIDTH), pl.BlockSpec((ATTN_WIDTH, tm), lambda i: (0, i))]
        out_shape += [jax.ShapeDtypeStruct((m, ATTN_WIDTH), BF16),
                      jax.ShapeDtypeStruct((ATTN_WIDTH, m), BF16)]
    return pl.pallas_call(
        functools.partial(_proj_kernel, q_scale=q_scale),
        grid=(m // tm,),
        in_specs=[row(d), small(d), _resident(w_in.shape),
                  small(HEAD_DIM), small(HEAD_DIM), small(HEAD_DIM), small(MLP_WIDTH)],
        out_specs=out_specs,
        out_shape=out_shape,
        compiler_params=_params("parallel"),
        name="proj",
    )(h, nw, w_in, b_f, qn, kn, vmn)


def _cumsum_kernel(lf_ref, f_ref):
    x = lf_ref[...]
    n = x.shape[0]
    row = lax.broadcasted_iota(jnp.int32, x.shape, 0)
    s = 1
    while s < n:
        x = x + jnp.where(row >= s, pltpu.roll(x, s, 0), 0.0)
        s *= 2
    f_ref[...] = x * LOG2E


def _cumsum(lf, seq):
    return pl.pallas_call(
        _cumsum_kernel,
        grid=(lf.shape[0] // seq,),
        in_specs=[pl.BlockSpec((seq, HEAD_DIM), lambda b: (b, 0))],
        out_specs=pl.BlockSpec((seq, HEAD_DIM), lambda b: (b, 0)),
        out_shape=jax.ShapeDtypeStruct(lf.shape, F32),
        compiler_params=_params("parallel"),
        name="cumsum",
    )(lf)


def _attn_kernel(q_ref, k_ref, vt_ref, f_ref, o_ref, fcol_ref, *, t, nq):
    h = pl.program_id(1)
    key = lax.broadcasted_iota(jnp.int32, (t, t), 0)
    qry = lax.broadcasted_iota(jnp.int32, (t, t), 1)
    nt = (((1,), (1,)), ((), ()))

    f = f_ref[...]
    lane = lax.broadcasted_iota(jnp.int32, f.shape, 1)
    fcol_ref[...] = jnp.sum(jnp.where(lane == h, f, 0.0), axis=-1, keepdims=True)

    def scores(q, lo, hi):
        return (lax.dot_general(k_ref[lo:hi, :], q, nt, preferred_element_type=F32)
                - fcol_ref[lo:hi, :])

    for blk in range(nq):
        past, n = blk * t, (blk + 1) * t
        q = q_ref[past:n, :]
        s_diag = jnp.where(key <= qry, scores(q, past, n), -jnp.inf)
        m = jnp.max(s_diag, axis=0, keepdims=True)
        if past:
            s_past = scores(q, 0, past)
            m = jnp.maximum(m, jnp.max(s_past, axis=0, keepdims=True))
        p = jnp.exp2(s_diag - m)
        l = jnp.sum(p, axis=0, keepdims=True)
        acc = _dot(vt_ref[:, past:n], p.astype(BF16))
        if past:
            p = jnp.exp2(s_past - m)
            l = l + jnp.sum(p, axis=0, keepdims=True)
            acc = acc + _dot(vt_ref[:, 0:past], p.astype(BF16))
        o_ref[past:n, :] = (acc / l).T.astype(o_ref.dtype)


def _attn(q, kb, vt, f, batch, seq):
    t = ATT_TILE
    nq = seq // t
    return pl.pallas_call(
        functools.partial(_attn_kernel, t=t, nq=nq),
        grid=(batch, N_HEADS),
        in_specs=[
            pl.BlockSpec((seq, HEAD_DIM), lambda b, h: (b, h)),
            pl.BlockSpec((seq, HEAD_DIM), lambda b, h: (b, h)),
            pl.BlockSpec((HEAD_DIM, seq), lambda b, h: (h, b)),
            pl.BlockSpec((seq, HEAD_DIM), lambda b, h: (b, 0)),
        ],
        out_specs=pl.BlockSpec((seq, HEAD_DIM), lambda b, h: (b, h)),
        out_shape=jax.ShapeDtypeStruct((batch * seq, ATTN_WIDTH), BF16),
        scratch_shapes=[pltpu.VMEM((seq, 1), F32)],
        compiler_params=_params("parallel", "arbitrary"),
        name="attn",
    )(q, kb, vt, f)


def _mixout_kernel(att_ref, u_ref, vm_ref, sw_ref, sbt_ref, wo_ref, h_ref, o_ref, mlp_ref):
    tm = att_ref.shape[0]
    r = lax.broadcasted_iota(jnp.int32, (CHUNK, CHUNK), 0)
    c = lax.broadcasted_iota(jnp.int32, (CHUNK, CHUNK), 1)
    for g in range(N_GROUPS):
        w = jnp.where(r >= c, sw_ref[g], 0.0).astype(BF16)
        bias = sbt_ref[:, g:g + 1]
        cols = slice(g * HEAD_DIM, (g + 1) * HEAD_DIM)
        for n in range(tm // CHUNK):
            rows = slice(n * CHUNK, (n + 1) * CHUNK)
            mixed = _dot(w, vm_ref[rows, cols].astype(BF16)) + bias
            mlp_ref[rows, cols] = (u_ref[rows, cols] * mixed).astype(BF16)
    o_ref[...] = (h_ref[...] + _dot(att_ref[...], wo_ref[0:ATTN_WIDTH, :])
                  + _dot(mlp_ref[...], wo_ref[ATTN_WIDTH:, :]))


def _mixout(att, u, vm, sw, sbt, wo, h, tm):
    m, d = h.shape
    row = lambda w: pl.BlockSpec((tm, w), lambda i: (i, 0))
    return pl.pallas_call(
        _mixout_kernel,
        grid=(m // tm,),
        in_specs=[row(ATTN_WIDTH), row(MLP_WIDTH), row(MLP_WIDTH), _resident(sw.shape),
                  _resident(sbt.shape), _resident(wo.shape), row(d)],
        out_specs=row(d),
        out_shape=jax.ShapeDtypeStruct((m, d), F32),
        scratch_shapes=[pltpu.VMEM((tm, MLP_WIDTH), BF16)],
        compiler_params=_params("parallel"),
        name="mixout",
    )(att, u, vm, sw, sbt, wo, h)


def _mixout_one_kernel(att_ref, u_ref, vm_ref, w00_ref, b0_ref, wo_ref, h_ref, o_ref):
    mlp = u_ref[...] * (w00_ref[...] * vm_ref[...] + b0_ref[...])
    o_ref[...] = (h_ref[...] + _dot(att_ref[...].astype(BF16), wo_ref[0:ATTN_WIDTH, :])
                  + _dot(mlp.astype(BF16), wo_ref[ATTN_WIDTH:, :]))


def _mixout_one(att, u, vm, w00, b0, wo, h):
    m, d = h.shape
    full = lambda a: pl.BlockSpec(a.shape, lambda i: (0,) * a.ndim)
    args = (att, u, vm, w00, b0, wo, h)
    return pl.pallas_call(
        _mixout_one_kernel,
        grid=(1,),
        in_specs=[full(a) for a in args],
        out_specs=pl.BlockSpec((m, d), lambda i: (0, 0)),
        out_shape=jax.ShapeDtypeStruct((m, d), F32),
        compiler_params=_params("arbitrary"),
        name="mixout_one",
    )(*args)


def _decode_pages(q, state, ck, cv, clf):
    m_old, l_old, acc_old, carry = state
    sub = lax.broadcasted_iota(jnp.int32, (N_HEADS, HEAD_DIM), 0)
    lane = lax.broadcasted_iota(jnp.int32, (N_HEADS, PAGE), 1)

    def head_rows(page_ref, h):
        return page_ref[pl.ds(h, PAGE, stride=N_HEADS), :].astype(BF16)

    def own_head(per_head):
        out = per_head(0)
        for h in range(1, N_HEADS):
            out = jnp.where(sub == h, per_head(h), out)
        return out

    logits = []
    for j in range(len(ck)):
        lf = clf[j][...]
        y = lf
        s = 1
        while s < PAGE:
            y = y + jnp.where(lane + s < PAGE, pltpu.roll(y, PAGE - s, 1), 0.0)
            s *= 2
        decay = y - lf + carry
        carry = carry + y[:, 0:1]
        scores = own_head(lambda h: lax.dot_general(
            q, head_rows(ck[j], h), (((1,), (1,)), ((), ())),
            preferred_element_type=F32))
        logits.append(scores + decay)

    m_new = m_old
    for lg in logits:
        m_new = jnp.maximum(m_new, jnp.max(lg, axis=-1, keepdims=True))
    alpha = jnp.exp(m_old - m_new)
    l = alpha * l_old
    acc = alpha * acc_old
    for j in range(len(cv)):
        pr = jnp.exp(logits[j] - m_new)
        l = l + jnp.sum(pr, axis=-1, keepdims=True)
        prb = pr.astype(BF16)
        acc = acc + own_head(lambda h: _dot(prb, head_rows(cv[j], h)))
    return m_new, l, acc, carry


def _decode_kernel(pt_ref, q_ref, kn_ref, vn_ref, lfn_ref, *refs, n_steps):
    g = DEC_PAGES
    ck, cv, clf = refs[0:g], refs[g:2 * g], refs[2 * g:3 * g]
    o_ref, m_ref, l_ref, acc_ref, c_ref = refs[3 * g:]
    step = pl.program_id(1)
    q = q_ref[...]

    @pl.when(step == 0)
    def _():
        kn = kn_ref[...].astype(BF16).astype(F32)
        m_ref[...] = jnp.sum(q.astype(F32) * kn, axis=-1, keepdims=True)
        l_ref[...] = jnp.ones_like(l_ref)
        acc_ref[...] = vn_ref[...]
        c_ref[...] = lfn_ref[...]

    state = (m_ref[...], l_ref[...], acc_ref[...], c_ref[...])
    m_ref[...], l_ref[...], acc_ref[...], c_ref[...] = _decode_pages(q, state, ck, cv, clf)

    @pl.when(step == n_steps - 1)
    def _():
        o_ref[...] = acc_ref[...] / l_ref[...]


def _decode(page_table, q, kn, vn, lfn, ck, cv, clft):
    nb, n_pages = page_table.shape
    g = DEC_PAGES
    n_steps = n_pages // g
    per_b = lambda b, s, pt: (b, 0, 0)

    def page(j):
        return lambda b, s, pt: (pt[b, n_pages - 1 - (s * g + j)], 0, 0)

    kv_spec = lambda j: pl.BlockSpec((None, PAGE * N_HEADS, HEAD_DIM), page(j))
    lf_spec = lambda j: pl.BlockSpec((None, N_HEADS, PAGE), page(j))
    head_block = pl.BlockSpec((None, N_HEADS, HEAD_DIM), per_b)
    grid_spec = pltpu.PrefetchScalarGridSpec(
        num_scalar_prefetch=1,
        grid=(nb, n_steps),
        in_specs=([head_block, head_block, head_block, pl.BlockSpec((None, N_HEADS, 1), per_b)]
                  + [kv_spec(j) for j in range(g)] + [kv_spec(j) for j in range(g)]
                  + [lf_spec(j) for j in range(g)]),
        out_specs=head_block,
        scratch_shapes=[pltpu.VMEM((N_HEADS, 1), F32), pltpu.VMEM((N_HEADS, 1), F32),
                        pltpu.VMEM((N_HEADS, HEAD_DIM), F32), pltpu.VMEM((N_HEADS, 1), F32)],
    )
    return pl.pallas_call(
        functools.partial(_decode_kernel, n_steps=n_steps),
        grid_spec=grid_spec,
        out_shape=jax.ShapeDtypeStruct((nb, N_HEADS, HEAD_DIM), F32),
        compiler_params=_params("parallel", "arbitrary"),
        name="decode",
    )(page_table, q, kn, vn, lfn, *([ck] * g), *([cv] * g), *([clft] * g))


def _ple_kernel(h_ref, p_ref, nw_ref, wg_ref, wp_ref, o_ref):
    h = h_ref[...]
    gate = jax.nn.sigmoid(_dot(_rms(h, nw_ref[...]).astype(BF16), wg_ref[...]))
    o_ref[...] = h + gate * _dot(p_ref[...].astype(BF16), wp_ref[...])


def _ple(h, p, nw, wg, wp, tm):
    m, d = h.shape
    return pl.pallas_call(
        _ple_kernel,
        grid=(m // tm,),
        in_specs=[pl.BlockSpec((tm, d), lambda i: (i, 0)),
                  pl.BlockSpec((tm, p.shape[1]), lambda i: (i, 0)),
                  pl.BlockSpec((1, d), lambda i: (0, 0)),
                  _resident(wg.shape), _resident(wp.shape)],
        out_specs=pl.BlockSpec((tm, d), lambda i: (i, 0)),
        out_shape=jax.ShapeDtypeStruct((m, d), F32),
        compiler_params=_params("parallel"),
        name="ple",
    )(h, p, nw, wg, wp)


def _pad_cols(w, mult):
    return jnp.pad(w, ((0, 0), (0, (-w.shape[1]) % mult)))


def kernel(x_prompt, x_sample, p_prompt, p_sample, cache_k, cache_v, cache_logf, page_table,
           ffn1_norm, ffn1_w_gate, ffn1_w_up, ffn1_w_down, mix_norm, w_in, b_f, q_norm, k_norm,
           v_mlp_norm, spatial_w, spatial_b, w_o, ffn2_norm, ffn2_w_gate, ffn2_w_up, ffn2_w_down,
           ple_norm, w_ple_gate, w_ple_proj):
    batch, seq, d = x_prompt.shape
    nb, dec_seq, _ = x_sample.shape
    depth = ffn1_norm.shape[0]
    m = batch * seq
    assert depth == 1 and dec_seq == 1
    assert seq % ATT_TILE == 0 and m % ROW_TILE == 0 and m % PROJ_TILE == 0
    assert page_table.shape[1] % DEC_PAGES == 0
    ffn_tm = FFN_ROW_TILE if m % FFN_ROW_TILE == 0 else ROW_TILE
    l = 0

    def ffn_weights(wg, wu, wd):
        return _cast_padded(wg, 1), _cast_padded(wu, 1), _cast_padded(wd, 0)

    ffn1 = (ffn1_norm[l][None],) + ffn_weights(ffn1_w_gate[l], ffn1_w_up[l], ffn1_w_down[l])
    ffn2 = (ffn2_norm[l][None],) + ffn_weights(ffn2_w_gate[l], ffn2_w_up[l], ffn2_w_down[l])
    proj = (mix_norm[l][None], _cast_padded(w_in[l], 1),
            _pad_cols(b_f[l][None], HEAD_DIM), q_norm[l][None], k_norm[l][None],
            v_mlp_norm[l].reshape(1, MLP_WIDTH))
    wo = w_o[l].astype(BF16)
    ple = (ple_norm[l][None], w_ple_gate[l].astype(BF16), w_ple_proj[l].astype(BF16))
    heads = (N_HEADS, HEAD_DIM)

    hp = _ffn(x_prompt.reshape(m, d), *ffn1, tm=ffn_tm)
    q, k, v, lf, u, vm, kb, vt = _proj(hp, *proj, tm=PROJ_TILE, q_scale=QK_SCALE * LOG2E,
                                       for_attn=True)
    att = _attn(q, kb, vt, _cumsum(lf, seq), batch, seq)
    hp = _mixout(att, u, vm, spatial_w[l], spatial_b[l].T, wo, hp, tm=ROW_TILE)
    hp = _ffn(hp, *ffn2, tm=ffn_tm)
    hp = _ple(hp, p_prompt[l].reshape(m, -1), *ple, tm=ROW_TILE)

    hs = _ffn(x_sample.reshape(nb, d), *ffn1, tm=nb)
    qs, ks, vs, lfs, us, vms = _proj(hs, *proj, tm=nb, q_scale=QK_SCALE, for_attn=False)
    att_s = _decode(page_table, qs.reshape(nb, *heads), ks.reshape(nb, *heads),
                    vs.reshape(nb, *heads), lfs[:, :N_HEADS].reshape(nb, N_HEADS, 1),
                    cache_k[l].reshape(-1, PAGE * N_HEADS, HEAD_DIM),
                    cache_v[l].reshape(-1, PAGE * N_HEADS, HEAD_DIM),
                    jnp.swapaxes(cache_logf[l], 1, 2))
    w00 = jnp.repeat(spatial_w[l][:, 0, 0], HEAD_DIM)[None]
    b0 = jnp.repeat(spatial_b[l][:, 0], HEAD_DIM)[None]
    hs = _mixout_one(att_s.reshape(nb, ATTN_WIDTH), us, vms, w00, b0, wo, hs)
    hs = _ffn(hs, *ffn2, tm=nb)
    hs = _ple(hs, p_sample[l].reshape(nb, -1), *ple, tm=nb)

    return (hp.reshape(batch, seq, d), hs.reshape(nb, 1, d),
            k.reshape(1, batch, seq, *heads), v.reshape(1, batch, seq, *heads),
            lf[:, :N_HEADS].reshape(1, batch, seq, N_HEADS),
            ks.reshape(1, nb, 1, *heads), vs.reshape(1, nb, 1, *heads),
            lfs[:, :N_HEADS].reshape(1, nb, 1, N_HEADS),
            vms.reshape(1, nb, 1, N_GROUPS, HEAD_DIM))
```

```python
import functools

import jax
import jax.numpy as jnp
from jax import lax
from jax.experimental import pallas as pl
from jax.experimental.pallas import tpu as pltpu

F32 = jnp.float32
BF16 = jnp.bfloat16

HEAD_DIM = 128
N_HEADS = 8
N_GROUPS = 8
ATTN_WIDTH = N_HEADS * HEAD_DIM
MLP_WIDTH = N_GROUPS * HEAD_DIM
CHUNK = 128
PAGE = 128
RMS_EPS = 1e-6
QK_SCALE = HEAD_DIM ** -0.5
LOG2E = 1.4426950408889634

V7X_VMEM_BYTES = 64 * 1024 * 1024
VMEM_LIMIT = V7X_VMEM_BYTES - 8 * 1024 * 1024

FF_TILE = 512
FFN_ROW_TILE = 1024
ROW_TILE = 512
PROJ_TILE = 512
ATT_TILE = 512
DEC_PAGES = 16


def _params(*sem):
    return pltpu.CompilerParams(dimension_semantics=sem, vmem_limit_bytes=VMEM_LIMIT)


def _resident(shape):
    zeros = (0,) * len(shape)
    return pl.BlockSpec(shape, lambda *_: zeros, pipeline_mode=pl.Buffered(1))


def _rms(x, w):
    return x * lax.rsqrt(jnp.mean(x * x, axis=-1, keepdims=True) + RMS_EPS) * w


def _dot(a, b):
    return jnp.dot(a, b, preferred_element_type=F32)


def _cast_kernel(w_ref, o_ref, *, axis, size, tile):
    pos = pl.program_id(0) * tile + lax.broadcasted_iota(jnp.int32, w_ref.shape, axis)
    o_ref[...] = jnp.where(pos < size, w_ref[...], 0.0).astype(BF16)


def _cast_padded(w, axis):
    size = w.shape[axis]
    n = pl.cdiv(size, FF_TILE)
    block = tuple(FF_TILE if a == axis else s for a, s in enumerate(w.shape))
    index = (lambda i: (i, 0)) if axis == 0 else (lambda i: (0, i))
    out_shape = tuple(n * FF_TILE if a == axis else s for a, s in enumerate(w.shape))
    return pl.pallas_call(
        functools.partial(_cast_kernel, axis=axis, size=size, tile=FF_TILE),
        grid=(n,),
        in_specs=[pl.BlockSpec(block, index)],
        out_specs=pl.BlockSpec(block, index),
        out_shape=jax.ShapeDtypeStruct(out_shape, BF16),
        compiler_params=_params("parallel"),
        name="cast",
    )(w)


def _ffn_kernel(x_ref, nw_ref, wg_ref, wu_ref, wd_ref, o_ref, xn_ref, *, nf):
    f = pl.program_id(1)

    def down(xn):
        g = _dot(xn, wg_ref[...])
        u = _dot(xn, wu_ref[...])
        h = (jax.nn.silu(g) * u).astype(BF16)
        return _dot(h, wd_ref[...])

    @pl.when(f == 0)
    def _():
        xn = _rms(x_ref[...], nw_ref[...]).astype(BF16)
        xn_ref[...] = xn
        o_ref[...] = down(xn)

    @pl.when(jnp.logical_and(f > 0, f < nf - 1))
    def _():
        o_ref[...] += down(xn_ref[...])

    @pl.when(f == nf - 1)
    def _():
        o_ref[...] = x_ref[...] + 0.5 * (o_ref[...] + down(xn_ref[...]))


def _ffn(x, nw, wg, wu, wd, tm):
    m, d = x.shape
    nf = wg.shape[1] // FF_TILE
    return pl.pallas_call(
        functools.partial(_ffn_kernel, nf=nf),
        grid=(m // tm, nf),
        in_specs=[
            pl.BlockSpec((tm, d), lambda i, f: (i, 0)),
            pl.BlockSpec((1, d), lambda i, f: (0, 0)),
            pl.BlockSpec((d, FF_TILE), lambda i, f: (0, f)),
            pl.BlockSpec((d, FF_TILE), lambda i, f: (0, f)),
            pl.BlockSpec((FF_TILE, d), lambda i, f: (f, 0)),
        ],
        out_specs=pl.BlockSpec((tm, d), lambda i, f: (i, 0)),
        out_shape=jax.ShapeDtypeStruct((m, d), F32),
        scratch_shapes=[pltpu.VMEM((tm, d), BF16)],
        compiler_params=_params("parallel", "arbitrary"),
        name="ffn",
    )(x, nw, wg, wu, wd)


def _group_rms(z, gain):
    outs = []
    for h in range(z.shape[1] // HEAD_DIM):
        sl = slice(h * HEAD_DIM, (h + 1) * HEAD_DIM)
        g = gain if gain.shape[1] == HEAD_DIM else gain[:, sl]
        outs.append(_rms(z[:, sl], g))
    return outs


def _proj_kernel(h_ref, nw_ref, w_ref, bf_ref, qn_ref, kn_ref, vmn_ref,
                 q_ref, k_ref, v_ref, lf_ref, u_ref, vm_ref, *attn_refs, q_scale):
    a = _rms(h_ref[...], nw_ref[...]).astype(BF16)
    aw = ATTN_WIDTH

    zq = _dot(a, w_ref[:, 0:aw])
    for h, qh in enumerate(_group_rms(zq, qn_ref[...])):
        q_ref[:, h * HEAD_DIM:(h + 1) * HEAD_DIM] = (qh * q_scale).astype(BF16)

    zk = _dot(a, w_ref[:, aw:2 * aw])
    for h, kh in enumerate(_group_rms(zk, kn_ref[...])):
        sl = slice(h * HEAD_DIM, (h + 1) * HEAD_DIM)
        k_ref[:, sl] = kh
        if attn_refs:
            attn_refs[0][:, sl] = kh.astype(BF16)

    zv = _dot(a, w_ref[:, 2 * aw:3 * aw])
    v_ref[...] = zv
    if attn_refs:
        attn_refs[1][...] = zv.T.astype(BF16)

    zu = _dot(a, w_ref[:, 3 * aw:3 * aw + MLP_WIDTH])
    u_ref[...] = jax.nn.gelu(zu).astype(u_ref.dtype)

    zm = _dot(a, w_ref[:, 3 * aw + MLP_WIDTH:3 * aw + 2 * MLP_WIDTH])
    for g, vg in enumerate(_group_rms(jax.nn.gelu(zm), vmn_ref[...])):
        vm_ref[:, g * HEAD_DIM:(g + 1) * HEAD_DIM] = vg.astype(vm_ref.dtype)

    n_main = 3 * aw + 2 * MLP_WIDTH
    zf = _dot(a, w_ref[:, n_main:n_main + HEAD_DIM]) + bf_ref[...]
    lf_ref[...] = jnp.minimum(zf, 0.0) - jnp.log1p(jnp.exp(-jnp.abs(zf)))


def _proj(h, nw, w_in, b_f, qn, kn, vmn, tm, q_scale, for_attn):
    m, d = h.shape
    row = lambda w: pl.BlockSpec((tm, w), lambda i: (i, 0))
    small = lambda w: pl.BlockSpec((1, w), lambda i: (0, 0))
    mlp_dt = BF16 if for_attn else F32
    out_w = [(ATTN_WIDTH, BF16), (ATTN_WIDTH, F32), (ATTN_WIDTH, F32), (HEAD_DIM, F32),
             (MLP_WIDTH, mlp_dt), (MLP_WIDTH, mlp_dt)]
    out_specs = [row(w) for w, _ in out_w]
    out_shape = [jax.ShapeDtypeStruct((m, w), dt) for w, dt in out_w]
    if for_attn:
        out_specs += [row(ATTN_WIDTH), pl.BlockSpec((ATTN_WIDTH, tm), lambda i: (0, i))]
        out_shape += [jax.ShapeDtypeStruct((m, ATTN_WIDTH), BF16),
                      jax.ShapeDtypeStruct((ATTN_WIDTH, m), BF16)]
    return pl.pallas_call(
        functools.partial(_proj_kernel, q_scale=q_scale),
        grid=(m // tm,),
        in_specs=[row(d), small(d), _resident(w_in.shape),
                  small(HEAD_DIM), small(HEAD_DIM), small(HEAD_DIM), small(MLP_WIDTH)],
        out_specs=out_specs,
        out_shape=out_shape,
        compiler_params=_params("parallel"),
        name="proj",
    )(h, nw, w_in, b_f, qn, kn, vmn)


def _cumsum_kernel(lf_ref, f_ref):
    x = lf_ref[...]
    n = x.shape[0]
    row = lax.broadcasted_iota(jnp.int32, x.shape, 0)
    s = 1
    while s < n:
        x = x + jnp.where(row >= s, pltpu.roll(x, s, 0), 0.0)
        s *= 2
    f_ref[...] = x * LOG2E


def _cumsum(lf, seq):
    return pl.pallas_call(
        _cumsum_kernel,
        grid=(lf.shape[0] // seq,),
        in_specs=[pl.BlockSpec((seq, HEAD_DIM), lambda b: (b, 0))],
        out_specs=pl.BlockSpec((seq, HEAD_DIM), lambda b: (b, 0)),
        out_shape=jax.ShapeDtypeStruct(lf.shape, F32),
        compiler_params=_params("parallel"),
        name="cumsum",
    )(lf)


def _attn_kernel(q_ref, k_ref, vt_ref, f_ref, o_ref, fcol_ref, *, t, nq):
    h = pl.program_id(1)
    key = lax.broadcasted_iota(jnp.int32, (t, t), 0)
    qry = lax.broadcasted_iota(jnp.int32, (t, t), 1)
    nt = (((1,), (1,)), ((), ()))

    f = f_ref[...]
    lane = lax.broadcasted_iota(jnp.int32, f.shape, 1)
    fcol_ref[...] = jnp.sum(jnp.where(lane == h, f, 0.0), axis=-1, keepdims=True)

    def scores(q, lo, hi):
        return (lax.dot_general(k_ref[lo:hi, :], q, nt, preferred_element_type=F32)
                - fcol_ref[lo:hi, :])

    for blk in range(nq):
        past, n = blk * t, (blk + 1) * t
        q = q_ref[past:n, :]
        s_diag = jnp.where(key <= qry, scores(q, past, n), -jnp.inf)
        m = jnp.max(s_diag, axis=0, keepdims=True)
        if past:
            s_past = scores(q, 0, past)
            m = jnp.maximum(m, jnp.max(s_past, axis=0, keepdims=True))
        p = jnp.exp2(s_diag - m)
        l = jnp.sum(p, axis=0, keepdims=True)
        acc = _dot(vt_ref[:, past:n], p.astype(BF16))
        if past:
            p = jnp.exp2(s_past - m)
            l = l + jnp.sum(p, axis=0, keepdims=True)
            acc = acc + _dot(vt_ref[:, 0:past], p.astype(BF16))
        o_ref[past:n, :] = (acc / l).T.astype(o_ref.dtype)


def _attn(q, kb, vt, f, batch, seq):
    t = ATT_TILE
    nq = seq // t
    return pl.pallas_call(
        functools.partial(_attn_kernel, t=t, nq=nq),
        grid=(batch, N_HEADS),
        in_specs=[
            pl.BlockSpec((seq, HEAD_DIM), lambda b, h: (b, h)),
            pl.BlockSpec((seq, HEAD_DIM), lambda b, h: (b, h)),
            pl.BlockSpec((HEAD_DIM, seq), lambda b, h: (h, b)),
            pl.BlockSpec((seq, HEAD_DIM), lambda b, h: (b, 0)),
        ],
        out_specs=pl.BlockSpec((seq, HEAD_DIM), lambda b, h: (b, h)),
        out_shape=jax.ShapeDtypeStruct((batch * seq, ATTN_WIDTH), BF16),
        scratch_shapes=[pltpu.VMEM((seq, 1), F32)],
        compiler_params=_params("parallel", "arbitrary"),
        name="attn",
    )(q, kb, vt, f)


def _mixout_kernel(att_ref, u_ref, vm_ref, sw_ref, sbt_ref, wo_ref, h_ref, o_ref, mlp_ref):
    tm = att_ref.shape[0]
    r = lax.broadcasted_iota(jnp.int32, (CHUNK, CHUNK), 0)
    c = lax.broadcasted_iota(jnp.int32, (CHUNK, CHUNK), 1)
    for g in range(N_GROUPS):
        w = jnp.where(r >= c, sw_ref[g], 0.0).astype(BF16)
        bias = sbt_ref[:, g:g + 1]
        cols = slice(g * HEAD_DIM, (g + 1) * HEAD_DIM)
        for n in range(tm // CHUNK):
            rows = slice(n * CHUNK, (n + 1) * CHUNK)
            mixed = _dot(w, vm_ref[rows, cols].astype(BF16)) + bias
            mlp_ref[rows, cols] = (u_ref[rows, cols] * mixed).astype(BF16)
    o_ref[...] = (h_ref[...] + _dot(att_ref[...], wo_ref[0:ATTN_WIDTH, :])
                  + _dot(mlp_ref[...], wo_ref[ATTN_WIDTH:, :]))


def _mixout(att, u, vm, sw, sbt, wo, h, tm):
    m, d = h.shape
    row = lambda w: pl.BlockSpec((tm, w), lambda i: (i, 0))
    return pl.pallas_call(
        _mixout_kernel,
        grid=(m // tm,),
        in_specs=[row(ATTN_WIDTH), row(MLP_WIDTH), row(MLP_WIDTH), _resident(sw.shape),
                  _resident(sbt.shape), _resident(wo.shape), row(d)],
        out_specs=row(d),
        out_shape=jax.ShapeDtypeStruct((m, d), F32),
        scratch_shapes=[pltpu.VMEM((tm, MLP_WIDTH), BF16)],
        compiler_params=_params("parallel"),
        name="mixout",
    )(att, u, vm, sw, sbt, wo, h)


def _mixout_one_kernel(att_ref, u_ref, vm_ref, w00_ref, b0_ref, wo_ref, h_ref, o_ref):
    mlp = u_ref[...] * (w00_ref[...] * vm_ref[...] + b0_ref[...])
    o_ref[...] = (h_ref[...] + _dot(att_ref[...].astype(BF16), wo_ref[0:ATTN_WIDTH, :])
                  + _dot(mlp.astype(BF16), wo_ref[ATTN_WIDTH:, :]))


def _mixout_one(att, u, vm, w00, b0, wo, h):
    m, d = h.shape
    full = lambda a: pl.BlockSpec(a.shape, lambda i: (0,) * a.ndim)
    args = (att, u, vm, w00, b0, wo, h)
    return pl.pallas_call(
        _mixout_one_kernel,
        grid=(1,),
        in_specs=[full(a) for a in args],
        out_specs=pl.BlockSpec((m, d), lambda i: (0, 0)),
        out_shape=jax.ShapeDtypeStruct((m, d), F32),
        compiler_params=_params("arbitrary"),
        name="mixout_one",
    )(*args)


def _decode_pages(q, state, ck, cv, clf):
    m_old, l_old, acc_old, carry = state
    sub = lax.broadcasted_iota(jnp.int32, (N_HEADS, HEAD_DIM), 0)
    lane = lax.broadcasted_iota(jnp.int32, (N_HEADS, PAGE), 1)

    def head_rows(page_ref, h):
        return page_ref[pl.ds(h, PAGE, stride=N_HEADS), :].astype(BF16)

    def own_head(per_head):
        out = per_head(0)
        for h in range(1, N_HEADS):
            out = jnp.where(sub == h, per_head(h), out)
        return out

    logits = []
    for j in range(len(ck)):
        lf = clf[j][...]
        y = lf
        s = 1
        while s < PAGE:
            y = y + jnp.where(lane + s < PAGE, pltpu.roll(y, PAGE - s, 1), 0.0)
            s *= 2
        decay = y - lf + carry
        carry = carry + y[:, 0:1]
        scores = own_head(lambda h: lax.dot_general(
            q, head_rows(ck[j], h), (((1,), (1,)), ((), ())),
            preferred_element_type=F32))
        logits.append(scores + decay)

    m_new = m_old
    for lg in logits:
        m_new = jnp.maximum(m_new, jnp.max(lg, axis=-1, keepdims=True))
    alpha = jnp.exp(m_old - m_new)
    l = alpha * l_old
    acc = alpha * acc_old
    for j in range(len(cv)):
        pr = jnp.exp(logits[j] - m_new)
        l = l + jnp.sum(pr, axis=-1, keepdims=True)
        prb = pr.astype(BF16)
        acc = acc + own_head(lambda h: _dot(prb, head_rows(cv[j], h)))
    return m_new, l, acc, carry


def _decode_kernel(pt_ref, q_ref, kn_ref, vn_ref, lfn_ref, *refs, n_steps):
    g = DEC_PAGES
    ck, cv, clf = refs[0:g], refs[g:2 * g], refs[2 * g:3 * g]
    o_ref, m_ref, l_ref, acc_ref, c_ref = refs[3 * g:]
    step = pl.program_id(1)
    q = q_ref[...]

    @pl.when(step == 0)
    def _():
        kn = kn_ref[...].astype(BF16).astype(F32)
        m_ref[...] = jnp.sum(q.astype(F32) * kn, axis=-1, keepdims=True)
        l_ref[...] = jnp.ones_like(l_ref)
        acc_ref[...] = vn_ref[...]
        c_ref[...] = lfn_ref[...]

    state = (m_ref[...], l_ref[...], acc_ref[...], c_ref[...])
    m_ref[...], l_ref[...], acc_ref[...], c_ref[...] = _decode_pages(q, state, ck, cv, clf)

    @pl.when(step == n_steps - 1)
    def _():
        o_ref[...] = acc_ref[...] / l_ref[...]


def _decode(page_table, q, kn, vn, lfn, ck, cv, clft):
    nb, n_pages = page_table.shape
    g = DEC_PAGES
    n_steps = n_pages // g
    per_b = lambda b, s, pt: (b, 0, 0)

    def page(j):
        return lambda b, s, pt: (pt[b, n_pages - 1 - (s * g + j)], 0, 0)

    kv_spec = lambda j: pl.BlockSpec((None, PAGE * N_HEADS, HEAD_DIM), page(j))
    lf_spec = lambda j: pl.BlockSpec((None, N_HEADS, PAGE), page(j))
    head_block = pl.BlockSpec((None, N_HEADS, HEAD_DIM), per_b)
    grid_spec = pltpu.PrefetchScalarGridSpec(
        num_scalar_prefetch=1,
        grid=(nb, n_steps),
        in_specs=([head_block, head_block, head_block, pl.BlockSpec((None, N_HEADS, 1), per_b)]
                  + [kv_spec(j) for j in range(g)] + [kv_spec(j) for j in range(g)]
                  + [lf_spec(j) for j in range(g)]),
        out_specs=head_block,
        scratch_shapes=[pltpu.VMEM((N_HEADS, 1), F32), pltpu.VMEM((N_HEADS, 1), F32),
                        pltpu.VMEM((N_HEADS, HEAD_DIM), F32), pltpu.VMEM((N_HEADS, 1), F32)],
    )
    return pl.pallas_call(
        functools.partial(_decode_kernel, n_steps=n_steps),
        grid_spec=grid_spec,
        out_shape=jax.ShapeDtypeStruct((nb, N_HEADS, HEAD_DIM), F32),
        compiler_params=_params("parallel", "arbitrary"),
        name="decode",
    )(page_table, q, kn, vn, lfn, *([ck] * g), *([cv] * g), *([clft] * g))


def _ple_kernel(h_ref, p_ref, nw_ref, wg_ref, wp_ref, o_ref):
    h = h_ref[...]
    gate = jax.nn.sigmoid(_dot(_rms(h, nw_ref[...]).astype(BF16), wg_ref[...]))
    o_ref[...] = h + gate * _dot(p_ref[...].astype(BF16), wp_ref[...])


def _ple(h, p, nw, wg, wp, tm):
    m, d = h.shape
    return pl.pallas_call(
        _ple_kernel,
        grid=(m // tm,),
        in_specs=[pl.BlockSpec((tm, d), lambda i: (i, 0)),
                  pl.BlockSpec((tm, p.shape[1]), lambda i: (i, 0)),
                  pl.BlockSpec((1, d), lambda i: (0, 0)),
                  _resident(wg.shape), _resident(wp.shape)],
        out_specs=pl.BlockSpec((tm, d), lambda i: (i, 0)),
        out_shape=jax.ShapeDtypeStruct((m, d), F32),
        compiler_params=_params("parallel"),
        name="ple",
    )(h, p, nw, wg, wp)


def _pad_cols(w, mult):
    return jnp.pad(w, ((0, 0), (0, (-w.shape[1]) % mult)))


def kernel(x_prompt, x_sample, p_prompt, p_sample, cache_k, cache_v, cache_logf, page_table,
           ffn1_norm, ffn1_w_gate, ffn1_w_up, ffn1_w_down, mix_norm, w_in, b_f, q_norm, k_norm,
           v_mlp_norm, spatial_w, spatial_b, w_o, ffn2_norm, ffn2_w_gate, ffn2_w_up, ffn2_w_down,
           ple_norm, w_ple_gate, w_ple_proj):
    batch, seq, d = x_prompt.shape
    nb, dec_seq, _ = x_sample.shape
    depth = ffn1_norm.shape[0]
    m = batch * seq
    assert depth == 1 and dec_seq == 1
    assert seq % ATT_TILE == 0 and m % ROW_TILE == 0 and m % PROJ_TILE == 0
    assert page_table.shape[1] % DEC_PAGES == 0
    ffn_tm = FFN_ROW_TILE if m % FFN_ROW_TILE == 0 else ROW_TILE
    l = 0

    def ffn_weights(wg, wu, wd):
        return _cast_padded(wg, 1), _cast_padded(wu, 1), _cast_padded(wd, 0)

    ffn1 = (ffn1_norm[l][None],) + ffn_weights(ffn1_w_gate[l], ffn1_w_up[l], ffn1_w_down[l])
    ffn2 = (ffn2_norm[l][None],) + ffn_weights(ffn2_w_gate[l], ffn2_w_up[l], ffn2_w_down[l])
    proj = (mix_norm[l][None], _cast_padded(w_in[l], 1),
            _pad_cols(b_f[l][None], HEAD_DIM), q_norm[l][None], k_norm[l][None],
            v_mlp_norm[l].reshape(1, MLP_WIDTH))
    wo = w_o[l].astype(BF16)
    ple = (ple_norm[l][None], w_ple_gate[l].astype(BF16), w_ple_proj[l].astype(BF16))
    heads = (N_HEADS, HEAD_DIM)

    hp = _ffn(x_prompt.reshape(m, d), *ffn1, tm=ffn_tm)
    q, k, v, lf, u, vm, kb, vt = _proj(hp, *proj, tm=PROJ_TILE, q_scale=QK_SCALE * LOG2E,
                                       for_attn=True)
    att = _attn(q, kb, vt, _cumsum(lf, seq), batch, seq)
    hp = _mixout(att, u, vm, spatial_w[l], spatial_b[l].T, wo, hp, tm=ROW_TILE)
    hp = _ffn(hp, *ffn2, tm=ffn_tm)
    hp = _ple(hp, p_prompt[l].reshape(m, -1), *ple, tm=ROW_TILE)

    hs = _ffn(x_sample.reshape(nb, d), *ffn1, tm=nb)
    qs, ks, vs, lfs, us, vms = _proj(hs, *proj, tm=nb, q_scale=QK_SCALE, for_attn=False)
    att_s = _decode(page_table, qs.reshape(nb, *heads), ks.reshape(nb, *heads),
                    vs.reshape(nb, *heads), lfs[:, :N_HEADS].reshape(nb, N_HEADS, 1),
                    cache_k[l].reshape(-1, PAGE * N_HEADS, HEAD_DIM),
                    cache_v[l].reshape(-1, PAGE * N_HEADS, HEAD_DIM),
                    jnp.swapaxes(cache_logf[l], 1, 2))
    w00 = jnp.repeat(spatial_w[l][:, 0, 0], HEAD_DIM)[None]
    b0 = jnp.repeat(spatial_b[l][:, 0], HEAD_DIM)[None]
    hs = _mixout_one(att_s.reshape(nb, ATTN_WIDTH), us, vms, w00, b0, wo, hs)
    hs = _ffn(hs, *ffn2, tm=nb)
    hs = _ple(hs, p_sample[l].reshape(nb, -1), *ple, tm=nb)

    return (hp.reshape(batch, seq, d), hs.reshape(nb, 1, d),
            k.reshape(1, batch, seq, *heads), v.reshape(1, batch, seq, *heads),
            lf[:, :N_HEADS].reshape(1, batch, seq, N_HEADS),
            ks.reshape(1, nb, 1, *heads), vs.reshape(1, nb, 1, *heads),
            lfs[:, :N_HEADS].reshape(1, nb, 1, N_HEADS),
            vms.reshape(1, nb, 1, N_GROUPS, HEAD_DIM))
```
